```python
import math
import jax, jax.numpy as jnp
from jax import lax
import numpy as np

D_MODEL = 1024
BATCH = 8
SEQ = 4096
DEPTH = 4

A_HEADS = 8
A_HEAD_DIM = 64
A_WIDTH = A_HEADS * A_HEAD_DIM
MOBA_BLOCK = 256
MOBA_TOPK = 3
B_HEADS = 4
B_HEAD_DIM = 128
B_WIDTH = B_HEADS * B_HEAD_DIM
CONV_WIDTH = 4
GDN_CHUNK = 64
C_HEADS = 8
C_HEAD_DIM = 64
C_WIDTH = C_HEADS * C_HEAD_DIM
KV_RANK = 256
IDX_HEADS = 8
IDX_DIM = 64
DSA_TOPK_MAX = 256
IDX_WEIGHT_SCALE = IDX_HEADS ** -0.5 * IDX_DIM ** -0.5
N_BUCKETS = 32
MAX_DISTANCE = 128
Q_BLOCK = 128
D_FF = 4 * D_MODEL
DEEPNORM_ALPHA = (2 * DEPTH) ** 0.25
DEEPNORM_BETA = (8 * DEPTH) ** -0.25
LN_EPS = 1e-5
RMS_EPS = 1e-6

IN_SPLIT = (A_WIDTH, A_WIDTH, A_WIDTH,
            B_WIDTH, B_WIDTH, B_WIDTH, B_WIDTH, B_HEADS, B_HEADS,
            C_WIDTH, KV_RANK, IDX_HEADS * IDX_DIM, IDX_DIM, IDX_HEADS,
            D_MODEL, D_MODEL, D_MODEL)
IN_WIDTH = sum(IN_SPLIT)

kernel_name = 'hybrid_moba_gdn_dsa_block'


def split_cols(x, sizes):
    return jnp.split(x, np.cumsum(sizes)[:-1].tolist(), axis=-1)


def layer_norm(x, g, b):
    xf = x.astype(jnp.float32)
    mu = jnp.mean(xf, axis=-1, keepdims=True)
    var = jnp.mean(jnp.square(xf - mu), axis=-1, keepdims=True)
    return ((xf - mu) * lax.rsqrt(var + LN_EPS) * g + b).astype(x.dtype)


def rms_norm(x, g):
    xf = x.astype(jnp.float32)
    return (xf * lax.rsqrt(jnp.mean(xf * xf, axis=-1, keepdims=True) + RMS_EPS) * g).astype(x.dtype)


def l2_normalize(x):
    xf = x.astype(jnp.float32)
    return xf * lax.rsqrt(jnp.sum(xf * xf, axis=-1, keepdims=True) + RMS_EPS)


def t5_bucket(dist):
    max_exact = N_BUCKETS // 2
    d = jnp.maximum(dist, 0)
    large = max_exact + (jnp.log(jnp.maximum(d, max_exact).astype(jnp.float32) / max_exact)
                         / math.log(MAX_DISTANCE / max_exact) * (N_BUCKETS - max_exact)).astype(jnp.int32)
    return jnp.where(d < max_exact, d, jnp.minimum(large, N_BUCKETS - 1))


def causal_conv_silu(x, w):
    y = lax.conv_general_dilated(x, w[:, None, :].astype(x.dtype), window_strides=(1,),
                                 padding=((CONV_WIDTH - 1, 0),),
                                 dimension_numbers=('NWC', 'WIO', 'NWC'),
                                 feature_group_count=x.shape[-1])
    return jax.nn.silu(y)


def moba_attention(q, k, v, bias_tab):
    B, T, H, dh = q.shape
    nb = -(-T // MOBA_BLOCK)
    pad = nb * MOBA_BLOCK - T
    nq = T // Q_BLOCK
    kk = min(MOBA_TOPK, nb)
    scale = dh ** -0.5

    def to_blocks(t):
        t = jnp.pad(t, ((0, 0), (0, pad), (0, 0), (0, 0)))
        return t.reshape(B, nb, MOBA_BLOCK, H, dh).transpose(0, 3, 1, 2, 4)

    k_blk = to_blocks(k)
    v_blk = to_blocks(v)
    k_mean = jnp.mean(k_blk.astype(jnp.float32), axis=3).astype(q.dtype)
    q_chunks = q.reshape(B, nq, Q_BLOCK, H, dh).transpose(1, 0, 3, 2, 4)
    bi = jnp.arange(B)[:, None, None, None]
    hi = jnp.arange(H)[None, :, None, None]
    blk_pos = jnp.arange(MOBA_BLOCK)

    def one_chunk(args):
        c, qc = args
        t = c * Q_BLOCK + jnp.arange(Q_BLOCK)
        cur = (c * Q_BLOCK) // MOBA_BLOCK
        gate = jnp.einsum('bhqd,bhnd->bhqn', qc, k_mean).astype(jnp.float32)
        gate = jnp.where(jnp.arange(nb) < cur, gate, -jnp.inf)
        _, sel = lax.top_k(gate, kk)
        k_sel = k_blk[bi, hi, sel]
        v_sel = v_blk[bi, hi, sel]
        dist_p = t[None, None, :, None, None] - (sel[..., None] * MOBA_BLOCK + blk_pos)
        s_p = jnp.einsum('bhqd,bhqjpd->bhqjp', qc, k_sel).astype(jnp.float32) * scale
        s_p = s_p + bias_tab[hi[..., None], t5_bucket(dist_p)]
        s_p = jnp.where((sel < cur)[..., None], s_p, -jnp.inf).reshape(B, H, Q_BLOCK, kk * MOBA_BLOCK)
        k_own = lax.dynamic_index_in_dim(k_blk, cur, axis=2, keepdims=False)
        v_own = lax.dynamic_index_in_dim(v_blk, cur, axis=2, keepdims=False)
        dist_o = t[:, None] - (cur * MOBA_BLOCK + blk_pos)[None, :]
        s_o = jnp.einsum('bhqd,bhpd->bhqp', qc, k_own).astype(jnp.float32) * scale
        s_o = jnp.where(dist_o >= 0, s_o + bias_tab[:, t5_bucket(dist_o)], -jnp.inf)
        p = jax.nn.softmax(jnp.concatenate([s_p, s_o], axis=-1), axis=-1).astype(v.dtype)
        p_p = p[..., :kk * MOBA_BLOCK].reshape(B, H, Q_BLOCK, kk, MOBA_BLOCK)
        return (jnp.einsum('bhqjp,bhqjpd->bhqd', p_p, v_sel)
                + jnp.einsum('bhqp,bhpd->bhqd', p[..., kk * MOBA_BLOCK:], v_own))

    o = lax.map(one_chunk, (jnp.arange(nq), q_chunks))
    return o.transpose(1, 0, 3, 2, 4).reshape(B, T, H, dh)


def gated_delta_rule(q, k, v, log_a, beta):
    B, T, H, dk = q.shape
    dv = v.shape[-1]
    C = GDN_CHUNK
    n = T // C

    def to_chunks(t):
        t = jnp.moveaxis(t.astype(jnp.float32), 2, 1)
        return t.reshape(B, H, n, C, *t.shape[3:])

    q = to_chunks(q) * dk ** -0.5
    k = to_chunks(k)
    v = to_chunks(v)
    beta = to_chunks(beta)
    g = jnp.cumsum(to_chunks(log_a), axis=-1)
    causal = jnp.tril(jnp.ones((C, C), dtype=bool))
    strict = jnp.tril(jnp.ones((C, C), dtype=bool), -1)
    diff = g[..., :, None] - g[..., None, :]
    decay = jnp.where(causal, jnp.exp(jnp.where(causal, diff, 0.0)), 0.0)
    kb = k * beta[..., None]
    m = jnp.where(strict, -jnp.einsum('bhncd,bhnsd->bhncs', kb, k) * decay, 0.0)
    eye = jnp.eye(C, dtype=jnp.float32)
    tinv = lax.linalg.triangular_solve(eye - m, jnp.broadcast_to(eye, m.shape),
                                       left_side=True, lower=True, unit_diagonal=True)
    u = jnp.einsum('bhncs,bhnse->bhnce', tinv, v * beta[..., None])
    w = jnp.einsum('bhncs,bhnsd->bhncd', tinv, kb * jnp.exp(g)[..., None])
    a_intra = jnp.einsum('bhncd,bhnsd->bhncs', q, k) * decay
    xs = tuple(jnp.moveaxis(t, 2, 0) for t in (q, k, u, w, g, a_intra))

    def step(S, inp):
        qc, kc, uc, wc, gc, ac = inp
        v_new = uc - jnp.einsum('bhcd,bhde->bhce', wc, S)
        o = (jnp.einsum('bhcd,bhde->bhce', qc * jnp.exp(gc)[..., None], S)
             + jnp.einsum('bhcs,bhse->bhce', ac, v_new))
        g_last = gc[..., -1]
        S = (S * jnp.exp(g_last)[..., None, None]
             + jnp.einsum('bhcd,bhce->bhde', kc * jnp.exp(g_last[..., None] - gc)[..., None], v_new))
        return S, o

    _, o = lax.scan(step, jnp.zeros((B, H, dk, dv), jnp.float32), xs)
    return o.transpose(1, 0, 3, 2, 4).reshape(B, T, H, dv)


def dsa_attention(q, ckv, qi, ki, wi, w_uk, w_uv, bias_tab):
    B, T, H, dh = q.shape
    R = ckv.shape[-1]
    nq = T // Q_BLOCK
    topk = min(DSA_TOPK_MAX, T // 4)
    q_lat = jnp.einsum('bthd,rhd->bthr', q, w_uk) * dh ** -0.5
    ql_c = q_lat.reshape(B, nq, Q_BLOCK, H, R).swapaxes(0, 1)
    qi_c = qi.reshape(B, nq, Q_BLOCK, IDX_HEADS, IDX_DIM).swapaxes(0, 1)
    wi_c = wi.reshape(B, nq, Q_BLOCK, IDX_HEADS).swapaxes(0, 1)
    bi = jnp.arange(B)[:, None, None]
    key_pos = jnp.arange(T)

    def one_chunk(args):
        c, qlc, qic, wic = args
        t = c * Q_BLOCK + jnp.arange(Q_BLOCK)
        score = jax.nn.relu(jnp.einsum('bqhd,bsd->bqhs', qic, ki))
        score = jnp.einsum('bqhs,bqh->bqs', score, wic).astype(jnp.float32)
        score = jnp.where(key_pos[None, None, :] <= t[None, :, None], score, -jnp.inf)
        _, sel = lax.top_k(score, topk)
        c_sel = ckv[bi, sel]
        s = jnp.einsum('bqhr,bqkr->bqhk', qlc, c_sel).astype(jnp.float32)
        s = s + jnp.moveaxis(bias_tab[:, t5_bucket(t[None, :, None] - sel)], 0, 2)
        s = jnp.where((sel <= t[None, :, None])[:, :, None, :], s, -jnp.inf)
        p = jax.nn.softmax(s, axis=-1).astype(ckv.dtype)
        return jnp.einsum('bqhk,bqkr->bqhr', p, c_sel)

    o_lat = lax.map(one_chunk, (jnp.arange(nq), ql_c, qi_c, wi_c))
    o_lat = o_lat.swapaxes(0, 1).reshape(B, T, H, R)
    return jnp.einsum('bthr,rhd->bthd', o_lat, w_uv)


def hybrid_mixer(x, rel_bias, w_in, conv_w, a_log, dt_bias, gdn_norm, kv_norm, idx_k_ln_g,
                 idx_k_ln_b, w_uk, w_uv, w_branch_a, w_branch_b, w_branch_c, w_out):
    B, T, _ = x.shape
    (q_a, k_a, v_a, q_b, k_b, v_b, z_b, a_b, b_b, q_c, ckv, qi, ki, wi,
     g_a, g_b, g_c) = split_cols(x @ w_in, IN_SPLIT)

    def heads(t, h):
        return t.reshape(B, T, h, -1)

    o_a = moba_attention(heads(q_a, A_HEADS), heads(k_a, A_HEADS), heads(v_a, A_HEADS),
                         rel_bias[:, :A_HEADS].T).reshape(B, T, A_WIDTH)
    q_b, k_b, v_b = jnp.split(causal_conv_silu(jnp.concatenate([q_b, k_b, v_b], axis=-1), conv_w), 3, axis=-1)
    log_a = -jnp.exp(a_log.astype(jnp.float32)) * jax.nn.softplus(a_b.astype(jnp.float32) + dt_bias)
    beta = jax.nn.sigmoid(b_b.astype(jnp.float32))
    o_b = gated_delta_rule(l2_normalize(heads(q_b, B_HEADS)), l2_normalize(heads(k_b, B_HEADS)),
                           heads(v_b, B_HEADS), log_a, beta)
    o_b = (rms_norm(o_b, gdn_norm) * jax.nn.silu(heads(z_b, B_HEADS).astype(jnp.float32)))
    o_b = o_b.astype(x.dtype).reshape(B, T, B_WIDTH)
    o_c = dsa_attention(heads(q_c, C_HEADS), rms_norm(ckv, kv_norm), heads(qi, IDX_HEADS),
                        layer_norm(ki, idx_k_ln_g, idx_k_ln_b), wi * IDX_WEIGHT_SCALE,
                        w_uk, w_uv, rel_bias[:, A_HEADS:].T).reshape(B, T, C_WIDTH)
    y = (jax.nn.sigmoid(g_a) * (o_a @ w_branch_a)
         + jax.nn.sigmoid(g_b) * (o_b @ w_branch_b)
         + jax.nn.sigmoid(g_c) * (o_c @ w_branch_c))
    return y @ w_out


def setup_inputs(seed: int = 0) -> dict:
    key = jax.random.key(seed)
    ks = jax.random.split(key, 24)
    f32 = jnp.float32
    L = DEPTH

    def nrm(k, shape, scale):
        return jax.random.normal(k, shape, f32) * scale

    dt = jnp.exp(jax.random.uniform(ks[4], (L, B_HEADS), f32, math.log(1e-3), math.log(1e-1)))
    return {
        'x': nrm(ks[0], (BATCH, SEQ, D_MODEL), 1.0),
        'rel_bias': nrm(ks[1], (N_BUCKETS, A_HEADS + C_HEADS), 0.2),
        'w_in': nrm(ks[2], (L, D_MODEL, IN_WIDTH), D_MODEL ** -0.5),
        'conv_w': nrm(ks[3], (L, CONV_WIDTH, 3 * B_WIDTH), CONV_WIDTH ** -0.5),
        'a_log': jnp.log(jax.random.uniform(ks[5], (L, B_HEADS), f32, 1.0, 16.0)),
        'dt_bias': dt + jnp.log(-jnp.expm1(-dt)),
        'gdn_norm': 1.0 + nrm(ks[6], (L, B_HEAD_DIM), 0.02),
        'kv_norm': 1.0 + nrm(ks[7], (L, KV_RANK), 0.02),
        'idx_k_ln_g': 1.0 + nrm(ks[8], (L, IDX_DIM), 0.02),
        'idx_k_ln_b': nrm(ks[9], (L, IDX_DIM), 0.02),
        'w_uk': nrm(ks[10], (L, KV_RANK, C_HEADS, C_HEAD_DIM), KV_RANK ** -0.5),
        'w_uv': nrm(ks[11], (L, KV_RANK, C_HEADS, C_HEAD_DIM), KV_RANK ** -0.5),
        'w_branch_a': nrm(ks[12], (L, A_WIDTH, D_MODEL), A_WIDTH ** -0.5),
        'w_branch_b': nrm(ks[13], (L, B_WIDTH, D_MODEL), B_WIDTH ** -0.5),
        'w_branch_c': nrm(ks[14], (L, C_WIDTH, D_MODEL), C_WIDTH ** -0.5),
        'w_out': nrm(ks[15], (L, D_MODEL, D_MODEL), D_MODEL ** -0.5 * DEEPNORM_BETA),
        'ln1_g': 1.0 + nrm(ks[16], (L, D_MODEL), 0.02),
        'ln1_b': nrm(ks[17], (L, D_MODEL), 0.02),
        'w_up': nrm(ks[18], (L, D_MODEL, D_FF), D_MODEL ** -0.5),
        'w_down': nrm(ks[19], (L, D_FF, D_MODEL), D_FF ** -0.5 * DEEPNORM_BETA),
        'ln2_g': 1.0 + nrm(ks[20], (L, D_MODEL), 0.02),
        'ln2_b': nrm(ks[21], (L, D_MODEL), 0.02),
    }


def reference(x, rel_bias, w_in, conv_w, a_log, dt_bias, gdn_norm, kv_norm, idx_k_ln_g, idx_k_ln_b,
              w_uk, w_uv, w_branch_a, w_branch_b, w_branch_c, w_out, ln1_g, ln1_b, w_up, w_down,
              ln2_g, ln2_b):
    for i in range(DEPTH):
        mix = hybrid_mixer(x, rel_bias, w_in[i], conv_w[i], a_log[i], dt_bias[i], gdn_norm[i],
                           kv_norm[i], idx_k_ln_g[i], idx_k_ln_b[i], w_uk[i], w_uv[i],
                           w_branch_a[i], w_branch_b[i], w_branch_c[i], w_out[i])
        x = layer_norm(DEEPNORM_ALPHA * x + mix, ln1_g[i], ln1_b[i])
        h = jnp.square(jax.nn.relu(x @ w_up[i]))
        x = layer_norm(DEEPNORM_ALPHA * x + h @ w_down[i], ln2_g[i], ln2_b[i])
    return x
```

```python
import functools
import math

import numpy as np
import jax
import jax.numpy as jnp
from jax import lax
from jax.experimental import pallas as pl
from jax.experimental.pallas import tpu as pltpu

D_MODEL = 1024
DEPTH = 4
A_HEADS = 8
A_HEAD_DIM = 64
A_WIDTH = A_HEADS * A_HEAD_DIM
MOBA_BLOCK = 256
MOBA_TOPK = 3
B_HEADS = 4
B_HEAD_DIM = 128
B_WIDTH = B_HEADS * B_HEAD_DIM
CONV_WIDTH = 4
GDN_CHUNK = 64
C_HEADS = 8
C_HEAD_DIM = 64
C_WIDTH = C_HEADS * C_HEAD_DIM
KV_RANK = 256
IDX_HEADS = 8
IDX_DIM = 64
DSA_TOPK_MAX = 256
IDX_WEIGHT_SCALE = IDX_HEADS ** -0.5 * IDX_DIM ** -0.5
N_BUCKETS = 32
MAX_DISTANCE = 128
D_FF = 4 * D_MODEL
DEEPNORM_ALPHA = (2 * DEPTH) ** 0.25
LN_EPS = 1e-5
RMS_EPS = 1e-6

IN_SPLIT = (A_WIDTH, A_WIDTH, A_WIDTH,
            B_WIDTH, B_WIDTH, B_WIDTH, B_WIDTH, B_HEADS, B_HEADS,
            C_WIDTH, KV_RANK, IDX_HEADS * IDX_DIM, IDX_DIM, IDX_HEADS,
            D_MODEL, D_MODEL, D_MODEL)

F32 = jnp.float32
BF16 = jnp.bfloat16
NEG = -1e9
LANES = 128
DSA_QB = 128
DSA_KW = 256
SMALL_W = 128
SMALL_KI = 0
SMALL_WI = IDX_DIM
SMALL_A = SMALL_WI + IDX_HEADS
SMALL_B = SMALL_A + B_HEADS
VMEM_LIMIT = 56 * 1024 * 1024
HIGHEST = lax.Precision.HIGHEST

KEY_LO = int(np.int32(np.uint32(0xFF7FFFFF) ^ np.uint32(0x7FFFFFFF)))
KEY_HI = 0x7F800000


def _cparams(sem):
    return pltpu.CompilerParams(dimension_semantics=sem, vmem_limit_bytes=VMEM_LIMIT)


def _dot(a, b):
    return jnp.dot(a, b, preferred_element_type=F32)


def _dot_nt(a, b):
    return lax.dot_general(a, b, (((1,), (1,)), ((), ())), preferred_element_type=F32)


def _hdot(a, b):
    return jnp.dot(a, b, preferred_element_type=F32, precision=HIGHEST)


def _hdot_nt(a, b):
    return lax.dot_general(a, b, (((1,), (1,)), ((), ())), preferred_element_type=F32,
                           precision=HIGHEST)


def _layer_norm(r, g, b):
    mu = jnp.mean(r, axis=-1, keepdims=True)
    d = r - mu
    var = jnp.mean(d * d, axis=-1, keepdims=True)
    return d * lax.rsqrt(var + LN_EPS) * g + b


def _sigmoid(x):
    return 1.0 / (1.0 + jnp.exp(-x))


def _t5_bucket_np(d):
    max_exact = N_BUCKETS // 2
    d = np.maximum(d, 0)
    large = max_exact + (np.log(np.maximum(d, max_exact).astype(np.float32) / max_exact)
                         / math.log(MAX_DISTANCE / max_exact) * (N_BUCKETS - max_exact)).astype(np.int32)
    return np.where(d < max_exact, d, np.minimum(large, N_BUCKETS - 1))


def _bucket_tiles(qb, w, offs):
    tq = np.arange(qb)[:, None]
    tk = np.arange(w)[None, :]
    tiles = []
    for off in offs:
        d = off + tq - tk
        tiles.append(np.where(d >= 0, _t5_bucket_np(d), -1))
    return np.stack(tiles).astype(np.int32)


def _bias_kernel(tab_ref, mi_ref, di_ref, mo_ref, do_ref):
    h = pl.program_id(0)

    def build(idx, col):
        acc = jnp.full(idx.shape, NEG, F32)
        for b in range(N_BUCKETS):
            acc = jnp.where(idx == b, tab_ref[col, b], acc)
        return acc

    mo_ref[0] = build(mi_ref[...], h)
    do_ref[...] = build(di_ref[...], A_HEADS + h)


def _bias_tiles(rel_bias):
    mi = jnp.asarray(_bucket_tiles(MOBA_BLOCK, MOBA_BLOCK, (0, MOBA_BLOCK, 2 * MOBA_BLOCK)))
    di = jnp.asarray(_bucket_tiles(DSA_QB, DSA_KW, (0, DSA_QB, 2 * DSA_QB, 3 * DSA_QB)))
    return pl.pallas_call(
        _bias_kernel,
        grid=(A_HEADS,),
        in_specs=[pl.BlockSpec(memory_space=pltpu.SMEM),
                  pl.BlockSpec(mi.shape, lambda h: (0, 0, 0)),
                  pl.BlockSpec(di.shape, lambda h: (0, 0, 0))],
        out_specs=[pl.BlockSpec((1,) + mi.shape, lambda h: (h, 0, 0, 0)),
                   pl.BlockSpec(di.shape, lambda h: (0, h, 0))],
        out_shape=[jax.ShapeDtypeStruct((A_HEADS,) + mi.shape, F32),
                   jax.ShapeDtypeStruct((di.shape[0], C_HEADS * DSA_QB, DSA_KW), F32)],
        compiler_params=_cparams(("arbitrary",)),
        name="bias_tiles",
    )(rel_bias.T, mi, di)


IN_SEGS = (("qkv_a", 3 * A_WIDTH, BF16), ("qkv_b", 3 * B_WIDTH, F32), ("z_b", B_WIDTH, F32),
           ("q_c", C_WIDTH, BF16), ("ckv", KV_RANK, F32), ("qi", IDX_HEADS * IDX_DIM, BF16),
           ("small", SMALL_W, F32), ("gates", 3 * D_MODEL, BF16))
IN_PAD_WIDTH = sum(s[1] for s in IN_SEGS)


def _prep_w_in(w):
    cs = np.cumsum((0,) + IN_SPLIT)

    def seg(i):
        return w[:, cs[i]:cs[i + 1]]

    pad = jnp.zeros((w.shape[0], SMALL_W - (IDX_DIM + IDX_HEADS + 2 * B_HEADS)), w.dtype)
    small = jnp.concatenate([seg(12), seg(13), seg(7), seg(8), pad], axis=1)
    return jnp.concatenate([w[:, cs[0]:cs[3]], w[:, cs[3]:cs[6]], seg(6), seg(9), seg(10), seg(11),
                            small, w[:, cs[14]:cs[17]]], axis=1).astype(BF16)


def _in_proj_kernel(x_ref, w_ref, *out_refs):
    xb = x_ref[...].astype(BF16)
    off = 0
    for (_, width, _), o_ref in zip(IN_SEGS, out_refs):
        for c0 in range(0, width, 512):
            cw = min(512, width - c0)
            o_ref[:, c0:c0 + cw] = _dot(xb, w_ref[:, off + c0:off + c0 + cw]).astype(o_ref.dtype)
        off += width


def _in_proj(x2, w_pad):
    n = x2.shape[0]
    tm = 512
    return pl.pallas_call(
        _in_proj_kernel,
        grid=(n // tm,),
        in_specs=[pl.BlockSpec((tm, D_MODEL), lambda i: (i, 0)),
                  pl.BlockSpec((D_MODEL, IN_PAD_WIDTH), lambda i: (0, 0), pipeline_mode=pl.Buffered(1))],
        out_specs=[pl.BlockSpec((tm, wd), lambda i: (i, 0)) for _, wd, _ in IN_SEGS],
        out_shape=[jax.ShapeDtypeStruct((n, wd), dt) for _, wd, dt in IN_SEGS],
        compiler_params=_cparams(("parallel",)),
        name="in_proj",
    )(x2, w_pad)


def _moba_kernel(q_ref, k_ref, v_ref, bias_ref, o_ref, kmean_ref, *, nb):
    i = pl.program_id(2)
    BS = MOBA_BLOCK

    @pl.when(i == 0)
    def _():
        kf = k_ref[0, 0].astype(F32)
        kmean_ref[...] = jnp.mean(kf.reshape(nb, BS, A_HEAD_DIM), axis=1)

    q = q_ref[0, 0]
    km = kmean_ref[...]
    km_hi = km.astype(BF16)
    km_lo = (km - km_hi.astype(F32)).astype(BF16)
    g = _dot_nt(km_hi, q) + _dot_nt(km_lo, q)
    blk = lax.broadcasted_iota(jnp.int32, (nb, BS), 0)
    valid = blk < i
    g = jnp.where(valid, g, -jnp.inf)
    rank = jnp.zeros((nb, BS), jnp.int32)
    for m in range(nb):
        gm = g[m:m + 1, :]
        beats = jnp.logical_or(gm > g, jnp.logical_and(gm == g, m < blk))
        rank = rank + jnp.where(beats, 1, 0)
    keep = jnp.logical_or(jnp.logical_and(valid, rank < MOBA_TOPK), blk == i)
    keepf = jnp.concatenate([jnp.where(keep, 0.0, NEG), jnp.full((LANES - nb, BS), NEG, F32)], axis=0)
    amask = keepf.T.astype(BF16)

    qs = q * (A_HEAD_DIM ** -0.5)
    blk_row = lax.broadcasted_iota(jnp.int32, (LANES, BS), 0)

    def body(n, carry):
        m, l, acc = carry
        rows = pl.ds(pl.multiple_of(n * BS, BS), BS)
        kn = k_ref[0, 0, rows, :]
        vn = v_ref[0, 0, rows, :]
        onehot = jnp.where(blk_row == n, 1.0, 0.0).astype(BF16)
        s = _dot_nt(qs, kn) + _dot(amask, onehot) + bias_ref[0, jnp.minimum(i - n, 2)]
        m_new = jnp.maximum(m, jnp.max(s, axis=-1, keepdims=True))
        alpha = jnp.exp(m - m_new)
        p = jnp.exp(s - m_new)
        l = alpha * l + jnp.sum(p, axis=-1, keepdims=True)
        acc = alpha * acc + _dot(p.astype(BF16), vn)
        return m_new, l, acc

    init = (jnp.full((BS, 1), -1e30, F32), jnp.zeros((BS, 1), F32), jnp.zeros((BS, A_HEAD_DIM), F32))
    _, l, acc = lax.fori_loop(0, i + 1, body, init)
    o_ref[0, 0] = (acc / l).astype(o_ref.dtype)


def _moba(q, k, v, bias):
    B, H, T, dh = q.shape
    nb = T // MOBA_BLOCK
    return pl.pallas_call(
        functools.partial(_moba_kernel, nb=nb),
        grid=(B, H, nb),
        in_specs=[pl.BlockSpec((1, 1, MOBA_BLOCK, dh), lambda b, h, i: (b, h, i, 0)),
                  pl.BlockSpec((1, 1, T, dh), lambda b, h, i: (b, h, 0, 0)),
                  pl.BlockSpec((1, 1, T, dh), lambda b, h, i: (b, h, 0, 0)),
                  pl.BlockSpec((1, 3, MOBA_BLOCK, MOBA_BLOCK), lambda b, h, i: (h, 0, 0, 0))],
        out_specs=pl.BlockSpec((1, 1, MOBA_BLOCK, dh), lambda b, h, i: (b, h, i, 0)),
        out_shape=jax.ShapeDtypeStruct((B, H, T, dh), BF16),
        scratch_shapes=[pltpu.VMEM((nb, dh), F32)],
        compiler_params=_cparams(("parallel", "parallel", "arbitrary")),
        name="moba",
    )(q, k, v, bias)


def _softplus(x):
    return jnp.maximum(x, 0.0) + jnp.log(1.0 + jnp.exp(-jnp.abs(x)))


def _gdn_kernel(x_ref, z_ref, ab_ref, abt_ref, cw_ref, alog_ref, dtb_ref, alogt_ref, dtbt_ref, gn_ref,
                o_ref, xpad_s, state_s, *, tt_rows):
    tt = pl.program_id(1)
    C = GDN_CHUNK
    dk = B_HEAD_DIM
    TT = tt_rows

    @pl.when(tt == 0)
    def _():
        state_s[...] = jnp.zeros_like(state_s)
        xpad_s[0:8, :] = jnp.zeros((8, 3 * B_WIDTH), F32)

    xpad_s[8:8 + TT, :] = x_ref[0]

    ab = ab_ref[0]
    abt = abt_ref[0]
    la_cols = -jnp.exp(alog_ref[...]) * _softplus(ab[:, 0:B_HEADS] + dtb_ref[...])
    be_cols = _sigmoid(ab[:, B_HEADS:2 * B_HEADS])
    la_rows = -jnp.exp(alogt_ref[...]) * _softplus(abt[0:B_HEADS, :] + dtbt_ref[...])

    ri = lax.broadcasted_iota(jnp.int32, (C, C), 0)
    ci = lax.broadcasted_iota(jnp.int32, (C, C), 1)
    causal = ci <= ri
    strict = ci < ri
    low = jnp.where(causal, 1.0, 0.0)
    upp = jnp.where(ri <= ci, 1.0, 0.0)
    eye = jnp.where(ci == ri, 1.0, 0.0)
    zpad = jnp.zeros((dk - C, dk), F32)

    def conv_silu(col0):
        acc = jnp.zeros((TT, dk), F32)
        for j in range(CONV_WIDTH):
            r0 = 8 - (CONV_WIDTH - 1) + j
            acc = acc + xpad_s[r0:r0 + TT, col0:col0 + dk] * cw_ref[j:j + 1, col0:col0 + dk]
        return acc * _sigmoid(acc)

    for h in range(B_HEADS):
        qh = conv_silu(h * dk)
        kh = conv_silu(B_WIDTH + h * dk)
        vh = conv_silu(2 * B_WIDTH + h * dk)
        qh = qh * lax.rsqrt(jnp.sum(qh * qh, axis=-1, keepdims=True) + RMS_EPS) * (dk ** -0.5)
        kh = kh * lax.rsqrt(jnp.sum(kh * kh, axis=-1, keepdims=True) + RMS_EPS)
        for c in range(TT // C):
            r0 = c * C
            q = qh[r0:r0 + C]
            k = kh[r0:r0 + C]
            v = vh[r0:r0 + C]
            la_r = la_rows[h:h + 1, r0:r0 + C]
            la_c = la_cols[r0:r0 + C, h:h + 1]
            be = be_cols[r0:r0 + C, h:h + 1]
            g_col = jnp.sum(low * la_r, axis=1, keepdims=True)
            g_row = jnp.sum(upp * la_c, axis=0, keepdims=True)
            diff = g_col - g_row
            decay = jnp.where(causal, jnp.exp(jnp.where(causal, diff, 0.0)), 0.0)
            kb = k * be
            mm = jnp.where(strict, -_hdot_nt(kb, k) * decay, 0.0)
            tinv = eye + mm
            pw = mm
            for _ in range(int(math.log2(C)) - 1):
                pw = _hdot(pw, pw)
                tinv = tinv + _hdot(tinv, pw)
            eg = jnp.exp(g_col)
            u = _hdot(tinv, v * be)
            w = _hdot(tinv, kb * eg)
            a_intra = _hdot_nt(q, k) * decay
            S = state_s[h]
            v_new = u - _hdot(w, S)
            o = _hdot(q * eg, S) + _hdot(a_intra, v_new)
            g_last = g_col[C - 1:C, :]
            kd = k * jnp.exp(g_last - g_col)
            kdt = jnp.concatenate([kd, zpad], axis=0).T
            state_s[h] = S * jnp.exp(g_last) + _hdot(kdt, jnp.concatenate([v_new, zpad], axis=0))
            z = z_ref[0, r0:r0 + C, h * dk:(h + 1) * dk]
            on = o * lax.rsqrt(jnp.mean(o * o, axis=-1, keepdims=True) + RMS_EPS) * gn_ref[...]
            o_ref[0, r0:r0 + C, h * dk:(h + 1) * dk] = (on * (z * _sigmoid(z))).astype(o_ref.dtype)

    xpad_s[0:8, :] = xpad_s[TT:TT + 8, :]


def _gdn(xqkv, z, ab, abt, conv_w, a_log, dt_bias, gdn_norm):
    B, T, _ = xqkv.shape
    TT = 256
    full = lambda shape: pl.BlockSpec(shape, lambda b, t: (0,) * len(shape))
    return pl.pallas_call(
        functools.partial(_gdn_kernel, tt_rows=TT),
        grid=(B, T // TT),
        in_specs=[pl.BlockSpec((1, TT, 3 * B_WIDTH), lambda b, t: (b, t, 0)),
                  pl.BlockSpec((1, TT, B_WIDTH), lambda b, t: (b, t, 0)),
                  pl.BlockSpec((1, TT, 2 * B_HEADS), lambda b, t: (b, t, 0)),
                  pl.BlockSpec((1, 2 * B_HEADS, TT), lambda b, t: (b, 0, t)),
                  full((CONV_WIDTH, 3 * B_WIDTH)),
                  full((1, B_HEADS)), full((1, B_HEADS)), full((B_HEADS, 1)), full((B_HEADS, 1)),
                  full((1, B_HEAD_DIM))],
        out_specs=pl.BlockSpec((1, TT, B_WIDTH), lambda b, t: (b, t, 0)),
        out_shape=jax.ShapeDtypeStruct((B, T, B_WIDTH), BF16),
        scratch_shapes=[pltpu.VMEM((TT + 8, 3 * B_WIDTH), F32),
                        pltpu.VMEM((B_HEADS, B_HEAD_DIM, B_HEAD_DIM), F32)],
        compiler_params=_cparams(("parallel", "arbitrary")),
        name="gdn",
    )(xqkv, z, ab, abt, conv_w, a_log.reshape(1, -1), dt_bias.reshape(1, -1),
      a_log.reshape(-1, 1), dt_bias.reshape(-1, 1), gdn_norm.reshape(1, -1))


def _dsa_prep_kernel(ckv_ref, small_ref, kvn_ref, lng_ref, lnb_ref, cn_ref, kin_ref):
    c = ckv_ref[...]
    ms = jnp.mean(c * c, axis=-1, keepdims=True)
    cn_ref[...] = (c * lax.rsqrt(ms + RMS_EPS) * kvn_ref[...]).astype(cn_ref.dtype)
    ki = small_ref[:, SMALL_KI:SMALL_KI + IDX_DIM]
    kin_ref[...] = _layer_norm(ki, lng_ref[...], lnb_ref[...]).astype(kin_ref.dtype)


def _dsa_prep(ckv, small, kv_norm, ln_g, ln_b):
    n = ckv.shape[0]
    tm = 1024
    return pl.pallas_call(
        _dsa_prep_kernel,
        grid=(n // tm,),
        in_specs=[pl.BlockSpec((tm, KV_RANK), lambda i: (i, 0)),
                  pl.BlockSpec((tm, SMALL_W), lambda i: (i, 0)),
                  pl.BlockSpec((1, KV_RANK), lambda i: (0, 0)),
                  pl.BlockSpec((1, IDX_DIM), lambda i: (0, 0)),
                  pl.BlockSpec((1, IDX_DIM), lambda i: (0, 0))],
        out_specs=[pl.BlockSpec((tm, KV_RANK), lambda i: (i, 0)),
                   pl.BlockSpec((tm, IDX_DIM), lambda i: (i, 0))],
        out_shape=[jax.ShapeDtypeStruct((n, KV_RANK), BF16), jax.ShapeDtypeStruct((n, IDX_DIM), BF16)],
        compiler_params=_cparams(("parallel",)),
        name="dsa_prep",
    )(ckv, small, kv_norm.reshape(1, -1), ln_g.reshape(1, -1), ln_b.reshape(1, -1))


def _key_to_f32(k):
    return lax.bitcast_convert_type(jnp.where(k >= 0, k, k ^ 0x7FFFFFFF), F32)


def _dsa_kernel(qc_ref, qi_ref, wi_ref, cn_ref, kin_ref, wuk_ref, wuv_ref, bias_ref, o_ref,
                sc_s, acc_s, *, topk):
    j = pl.program_id(1)
    QB, KW, H, R = DSA_QB, DSA_KW, C_HEADS, KV_RANK
    nck = (j * QB + QB - 1) // KW + 1

    qlat = _dot(qc_ref[0], wuk_ref[...])
    qs = jnp.concatenate([qlat[:, h * R:(h + 1) * R] for h in range(H)], axis=0).astype(BF16)

    qi = qi_ref[0, 0]
    wcol = wi_ref[0, 0] * IDX_WEIGHT_SCALE
    tq = j * QB + lax.broadcasted_iota(jnp.int32, (QB, KW), 0)
    tk0 = lax.broadcasted_iota(jnp.int32, (QB, KW), 1)

    def idx_body(c, _):
        rows = pl.ds(pl.multiple_of(c * KW, KW), KW)
        s8 = jnp.maximum(_dot_nt(qi, kin_ref[0, rows, :]), 0.0) * wcol
        sc = s8[0:QB]
        for h in range(1, IDX_HEADS):
            sc = sc + s8[h * QB:(h + 1) * QB]
        sc_s[c] = jnp.where(tk0 + c * KW <= tq, sc, -jnp.inf)
        return 0

    lax.fori_loop(0, nck, idx_body, 0)

    trow = j * QB + lax.broadcasted_iota(jnp.int32, (QB, 1), 0)
    kk = jnp.minimum(trow + 1, topk)

    def bis_body(_, carry):
        lo, hi, cge, cgt = carry
        mid = (lo >> 1) + (hi >> 1) + (lo & hi & 1) + ((lo ^ hi) & 1)
        midb = jnp.broadcast_to(_key_to_f32(mid), (QB, KW))

        def cnt_body(c, acc):
            return acc + jnp.where(sc_s[c] >= midb, 1, 0)

        acc = lax.fori_loop(0, nck, cnt_body, jnp.zeros((QB, KW), jnp.int32))
        cnt = jnp.sum(acc, axis=-1, keepdims=True)
        ok = cnt >= kk
        return (jnp.where(ok, mid, lo), jnp.where(ok, hi, mid - 1),
                jnp.where(ok, cnt, cge), jnp.where(ok, cgt, cnt))

    init = (jnp.full((QB, 1), KEY_LO, jnp.int32), jnp.full((QB, 1), KEY_HI, jnp.int32),
            trow + 1, jnp.zeros((QB, 1), jnp.int32))
    lo, _, cge, cgt = lax.fori_loop(0, 32, bis_body, init)
    taub = jnp.broadcast_to(_key_to_f32(lo), (QB, KW))

    @pl.when(jnp.max(cge - kk) > 0)
    def _():
        need = (kk - cgt).astype(F32)
        ur = lax.broadcasted_iota(jnp.int32, (KW, KW), 0)
        uc = lax.broadcasted_iota(jnp.int32, (KW, KW), 1)
        upper = jnp.where(ur <= uc, 1.0, 0.0).astype(BF16)

        def tie_body(c, seen):
            s = sc_s[c]
            eq = s == taub
            pc = _dot(jnp.where(eq, 1.0, 0.0).astype(BF16), upper) + seen
            sc_s[c] = jnp.where(jnp.logical_and(eq, pc > need), -jnp.inf, s)
            return pc[:, KW - 1:KW]

        lax.fori_loop(0, nck, tie_body, jnp.zeros((QB, 1), F32))

    acc_s[...] = jnp.zeros_like(acc_s)

    def att_body(c, carry):
        m, l = carry
        rows = pl.ds(pl.multiple_of(c * KW, KW), KW)
        cc = cn_ref[0, rows, :]
        madd = jnp.where(sc_s[c] >= taub, 0.0, NEG)
        bias = bias_ref[jnp.minimum((j * QB - c * KW) // QB, 3)]
        s = _dot_nt(qs, cc).reshape(H, QB, KW) + (bias.reshape(H, QB, KW) + madd[None])
        s = s.reshape(H * QB, KW)
        m_new = jnp.maximum(m, jnp.max(s, axis=-1, keepdims=True))
        alpha = jnp.exp(m - m_new)
        p = jnp.exp(s - m_new)
        l = alpha * l + jnp.sum(p, axis=-1, keepdims=True)
        acc_s[...] = alpha * acc_s[...] + _dot(p.astype(BF16), cc)
        return m_new, l

    init = (jnp.full((H * QB, 1), -1e30, F32), jnp.zeros((H * QB, 1), F32))
    _, l = lax.fori_loop(0, nck, att_body, init)
    o_lat = (acc_s[...] / l).astype(BF16)
    o2 = jnp.concatenate([o_lat[h * QB:(h + 1) * QB] for h in range(H)], axis=1)
    o_ref[0] = _dot(o2, wuv_ref[...]).astype(o_ref.dtype)


def _dsa(q_c, qi_st, wi_st, cn, kin, wuk_bd, wuv_bd, bias):
    B, T, _ = q_c.shape
    QB, KW = DSA_QB, DSA_KW
    nq = T // QB
    topk = min(DSA_TOPK_MAX, T // 4)
    return pl.pallas_call(
        functools.partial(_dsa_kernel, topk=topk),
        grid=(B, nq),
        in_specs=[pl.BlockSpec((1, QB, C_WIDTH), lambda b, j: (b, j, 0)),
                  pl.BlockSpec((1, 1, C_HEADS * QB, IDX_DIM), lambda b, j: (b, j, 0, 0)),
                  pl.BlockSpec((1, 1, C_HEADS * QB, 1), lambda b, j: (b, j, 0, 0)),
                  pl.BlockSpec((1, T, KV_RANK), lambda b, j: (b, 0, 0)),
                  pl.BlockSpec((1, T, IDX_DIM), lambda b, j: (b, 0, 0)),
                  pl.BlockSpec(wuk_bd.shape, lambda b, j: (0, 0)),
                  pl.BlockSpec(wuv_bd.shape, lambda b, j: (0, 0)),
                  pl.BlockSpec(bias.shape, lambda b, j: (0, 0, 0))],
        out_specs=pl.BlockSpec((1, QB, C_WIDTH), lambda b, j: (b, j, 0)),
        out_shape=jax.ShapeDtypeStruct((B, T, C_WIDTH), BF16),
        scratch_shapes=[pltpu.VMEM((T // KW, QB, KW), F32),
                        pltpu.VMEM((C_HEADS * QB, KV_RANK), F32)],
        compiler_params=_cparams(("parallel", "arbitrary")),
        name="dsa",
    )(q_c, qi_st, wi_st, cn, kin, wuk_bd, wuv_bd, bias)


def _merge_kernel(x_ref, oa_ref, ob_ref, oc_ref, g_ref, wa_ref, wb_ref, wc_ref, wo_ref, lg_ref, lb_ref,
                  out_ref):
    D = D_MODEL
    y = _sigmoid(g_ref[:, 0:D].astype(F32)) * _dot(oa_ref[...], wa_ref[...])
    y = y + _sigmoid(g_ref[:, D:2 * D].astype(F32)) * _dot(ob_ref[...], wb_ref[...])
    y = y + _sigmoid(g_ref[:, 2 * D:3 * D].astype(F32)) * _dot(oc_ref[...], wc_ref[...])
    mix = _dot(y.astype(BF16), wo_ref[...])
    out_ref[...] = _layer_norm(DEEPNORM_ALPHA * x_ref[...] + mix, lg_ref[...], lb_ref[...])


def _merge(x2, oa, ob, oc, gates, wa, wb, wc, wo, ln_g, ln_b):
    n = x2.shape[0]
    tm = 512
    row = lambda w: pl.BlockSpec((tm, w), lambda i: (i, 0))
    full = lambda a: pl.BlockSpec(a.shape, lambda i: (0, 0))
    lg, lb = ln_g.reshape(1, -1), ln_b.reshape(1, -1)
    return pl.pallas_call(
        _merge_kernel,
        grid=(n // tm,),
        in_specs=[row(D_MODEL), row(A_WIDTH), row(B_WIDTH), row(C_WIDTH), row(3 * D_MODEL),
                  full(wa), full(wb), full(wc), full(wo), full(lg), full(lb)],
        out_specs=row(D_MODEL),
        out_shape=jax.ShapeDtypeStruct((n, D_MODEL), F32),
        compiler_params=_cparams(("parallel",)),
        name="merge",
    )(x2, oa, ob, oc, gates, wa, wb, wc, wo, lg, lb)


def _mlp_kernel(x_ref, wu_ref, wd_ref, lg_ref, lb_ref, out_ref):
    x = x_ref[...]
    xb = x.astype(BF16)
    acc = jnp.zeros(x.shape, F32)
    fc = 1024
    for c in range(D_FF // fc):
        h = jnp.maximum(_dot(xb, wu_ref[:, c * fc:(c + 1) * fc]), 0.0)
        acc = acc + _dot((h * h).astype(BF16), wd_ref[c * fc:(c + 1) * fc, :])
    out_ref[...] = _layer_norm(DEEPNORM_ALPHA * x + acc, lg_ref[...], lb_ref[...])


def _mlp(x2, wu, wd, ln_g, ln_b):
    n = x2.shape[0]
    tm = 512
    lg, lb = ln_g.reshape(1, -1), ln_b.reshape(1, -1)
    return pl.pallas_call(
        _mlp_kernel,
        grid=(n // tm,),
        in_specs=[pl.BlockSpec((tm, D_MODEL), lambda i: (i, 0)),
                  pl.BlockSpec(wu.shape, lambda i: (0, 0), pipeline_mode=pl.Buffered(1)),
                  pl.BlockSpec(wd.shape, lambda i: (0, 0), pipeline_mode=pl.Buffered(1)),
                  pl.BlockSpec(lg.shape, lambda i: (0, 0)),
                  pl.BlockSpec(lb.shape, lambda i: (0, 0))],
        out_specs=pl.BlockSpec((tm, D_MODEL), lambda i: (i, 0)),
        out_shape=jax.ShapeDtypeStruct((n, D_MODEL), F32),
        compiler_params=_cparams(("parallel",)),
        name="mlp",
    )(x2, wu, wd, lg, lb)


def _block_diag_uk(w_uk):
    R, H, dh = w_uk.shape
    eye = jnp.eye(H, dtype=w_uk.dtype)
    bd = jnp.einsum("rhd,hg->hdgr", w_uk * (dh ** -0.5), eye)
    return bd.reshape(H * dh, H * R).astype(BF16)


def _block_diag_uv(w_uv):
    R, H, dh = w_uv.shape
    eye = jnp.eye(H, dtype=w_uv.dtype)
    bd = jnp.einsum("rhd,hg->hrgd", w_uv, eye)
    return bd.reshape(H * R, H * dh).astype(BF16)


def _mixer_branches(x2, B, T, w_pad, conv_w, a_log, dt_bias, gdn_norm, kv_norm, ln_g, ln_b,
                    wuk_bd, wuv_bd, bias_a, bias_c):
    qkv_a, qkv_b, z_b, q_c, ckv, qi, small, gates = _in_proj(x2, w_pad)

    qkv = qkv_a.reshape(B, T, 3, A_HEADS, A_HEAD_DIM).transpose(2, 0, 3, 1, 4)
    o_a = _moba(qkv[0], qkv[1], qkv[2], bias_a)
    o_a = o_a.transpose(0, 2, 1, 3).reshape(B * T, A_WIDTH)

    ab = small[:, SMALL_A:SMALL_A + 2 * B_HEADS].reshape(B, T, 2 * B_HEADS)
    o_b = _gdn(qkv_b.reshape(B, T, -1), z_b.reshape(B, T, -1), ab, ab.transpose(0, 2, 1),
               conv_w, a_log, dt_bias, gdn_norm).reshape(B * T, B_WIDTH)

    nq = T // DSA_QB
    cn, kin = _dsa_prep(ckv, small, kv_norm, ln_g, ln_b)
    qi_st = (qi.reshape(B, nq, DSA_QB, IDX_HEADS, IDX_DIM).transpose(0, 1, 3, 2, 4)
             .reshape(B, nq, IDX_HEADS * DSA_QB, IDX_DIM))
    wi_st = (small[:, SMALL_WI:SMALL_WI + IDX_HEADS].reshape(B, nq, DSA_QB, IDX_HEADS)
             .transpose(0, 1, 3, 2).reshape(B, nq, IDX_HEADS * DSA_QB, 1))
    o_c = _dsa(q_c.reshape(B, T, -1), qi_st, wi_st, cn.reshape(B, T, -1), kin.reshape(B, T, -1),
               wuk_bd, wuv_bd, bias_c).reshape(B * T, C_WIDTH)
    return o_a, o_b, o_c, gates


def kernel(x, rel_bias, w_in, conv_w, a_log, dt_bias, gdn_norm, kv_norm, idx_k_ln_g, idx_k_ln_b, w_uk, w_uv, w_branch_a, w_branch_b, w_branch_c, w_out, ln1_g, ln1_b, w_up, w_down, ln2_g, ln2_b):
    B, T, D = x.shape
    bias_a, bias_c = _bias_tiles(rel_bias)
    x2 = x.reshape(B * T, D)
    for i in range(w_in.shape[0]):
        o_a, o_b, o_c, gates = _mixer_branches(
            x2, B, T, _prep_w_in(w_in[i]), conv_w[i], a_log[i], dt_bias[i], gdn_norm[i], kv_norm[i],
            idx_k_ln_g[i], idx_k_ln_b[i], _block_diag_uk(w_uk[i]), _block_diag_uv(w_uv[i]),
            bias_a, bias_c)
        x2 = _merge(x2, o_a, o_b, o_c, gates, w_branch_a[i].astype(BF16), w_branch_b[i].astype(BF16),
                    w_branch_c[i].astype(BF16), w_out[i].astype(BF16), ln1_g[i], ln1_b[i])
        x2 = _mlp(x2, w_up[i].astype(BF16), w_down[i].astype(BF16), ln2_g[i], ln2_b[i])
    return x2.reshape(B, T, D)
```

```python
import functools
import math

import numpy as np
import jax
import jax.numpy as jnp
from jax import lax
from jax.experimental import pallas as pl
from jax.experimental.pallas import tpu as pltpu

D_MODEL = 1024
DEPTH = 4
A_HEADS = 8
A_HEAD_DIM = 64
A_WIDTH = A_HEADS * A_HEAD_DIM
MOBA_BLOCK = 256
MOBA_TOPK = 3
B_HEADS = 4
B_HEAD_DIM = 128
B_WIDTH = B_HEADS * B_HEAD_DIM
CONV_WIDTH = 4
GDN_CHUNK = 64
C_HEADS = 8
C_HEAD_DIM = 64
C_WIDTH = C_HEADS * C_HEAD_DIM
KV_RANK = 256
IDX_HEADS = 8
IDX_DIM = 64
DSA_TOPK_MAX = 256
IDX_WEIGHT_SCALE = IDX_HEADS ** -0.5 * IDX_DIM ** -0.5
N_BUCKETS = 32
MAX_DISTANCE = 128
D_FF = 4 * D_MODEL
DEEPNORM_ALPHA = (2 * DEPTH) ** 0.25
LN_EPS = 1e-5
RMS_EPS = 1e-6

IN_SPLIT = (A_WIDTH, A_WIDTH, A_WIDTH,
            B_WIDTH, B_WIDTH, B_WIDTH, B_WIDTH, B_HEADS, B_HEADS,
            C_WIDTH, KV_RANK, IDX_HEADS * IDX_DIM, IDX_DIM, IDX_HEADS,
            D_MODEL, D_MODEL, D_MODEL)

F32 = jnp.float32
BF16 = jnp.bfloat16
NEG = -1e9
M_INIT = -1e8
LANES = 128
DSA_QB = 128
DSA_KW = 256
SMALL_W = 128
SMALL_KI = 0
SMALL_WI = IDX_DIM
SMALL_A = SMALL_WI + IDX_HEADS
SMALL_B = SMALL_A + B_HEADS
VMEM_LIMIT = 56 * 1024 * 1024
HIGHEST = lax.Precision.HIGHEST

KEY_LO = int(np.int32(np.uint32(0xFF7FFFFF) ^ np.uint32(0x7FFFFFFF)))
KEY_HI = 0x7F800000


def _cparams(sem):
    return pltpu.CompilerParams(dimension_semantics=sem, vmem_limit_bytes=VMEM_LIMIT)


def _dot(a, b):
    return jnp.dot(a, b, preferred_element_type=F32)


def _dot_nt(a, b):
    return lax.dot_general(a, b, (((1,), (1,)), ((), ())), preferred_element_type=F32)


def _hdot(a, b):
    return jnp.dot(a, b, preferred_element_type=F32, precision=HIGHEST)


def _hdot_nt(a, b):
    return lax.dot_general(a, b, (((1,), (1,)), ((), ())), preferred_element_type=F32,
                           precision=HIGHEST)


def _layer_norm(r, g, b):
    mu = jnp.mean(r, axis=-1, keepdims=True)
    d = r - mu
    var = jnp.mean(d * d, axis=-1, keepdims=True)
    return d * lax.rsqrt(var + LN_EPS) * g + b


def _sigmoid(x):
    return 1.0 / (1.0 + jnp.exp(-x))


def _t5_bucket_np(d):
    max_exact = N_BUCKETS // 2
    d = np.maximum(d, 0)
    large = max_exact + (np.log(np.maximum(d, max_exact).astype(np.float32) / max_exact)
                         / math.log(MAX_DISTANCE / max_exact) * (N_BUCKETS - max_exact)).astype(np.int32)
    return np.where(d < max_exact, d, np.minimum(large, N_BUCKETS - 1))


def _bucket_tiles(qb, w, offs):
    tq = np.arange(qb)[:, None]
    tk = np.arange(w)[None, :]
    tiles = []
    for off in offs:
        d = off + tq - tk
        tiles.append(np.where(d >= 0, _t5_bucket_np(d), -1))
    return np.stack(tiles).astype(np.int32)


def _bias_kernel(tab_ref, mi_ref, di_ref, mo_ref, do_ref):
    h = pl.program_id(0)

    def build(idx, col):
        acc = jnp.full(idx.shape, NEG, F32)
        for b in range(N_BUCKETS):
            acc = jnp.where(idx == b, tab_ref[col, b], acc)
        return acc

    mo_ref[...] = build(mi_ref[...], h)
    do_ref[...] = build(di_ref[...], A_HEADS + h)


def _bias_tiles(rel_bias):
    mi = jnp.asarray(_bucket_tiles(MOBA_BLOCK, MOBA_BLOCK, (0, MOBA_BLOCK)).transpose(0, 2, 1))
    di = jnp.asarray(_bucket_tiles(DSA_QB, DSA_KW, (0, DSA_QB, 2 * DSA_QB, 3 * DSA_QB)))
    return pl.pallas_call(
        _bias_kernel,
        grid=(A_HEADS,),
        in_specs=[pl.BlockSpec(memory_space=pltpu.SMEM),
                  pl.BlockSpec(mi.shape, lambda h: (0, 0, 0)),
                  pl.BlockSpec(di.shape, lambda h: (0, 0, 0))],
        out_specs=[pl.BlockSpec(mi.shape, lambda h: (0, 0, h)),
                   pl.BlockSpec(di.shape, lambda h: (0, h, 0))],
        out_shape=[jax.ShapeDtypeStruct(mi.shape[:2] + (A_HEADS * mi.shape[2],), F32),
                   jax.ShapeDtypeStruct((di.shape[0], C_HEADS * DSA_QB, DSA_KW), F32)],
        compiler_params=_cparams(("arbitrary",)),
        name="bias_tiles",
    )(rel_bias.T, mi, di)


IN_SEGS = (("qkv_a", 3 * A_WIDTH, BF16), ("qkv_b", 3 * B_WIDTH, F32), ("z_b", B_WIDTH, F32),
           ("q_c", C_WIDTH, BF16), ("ckv", KV_RANK, F32), ("qi", IDX_HEADS * IDX_DIM, BF16),
           ("small", SMALL_W, F32), ("gates", 3 * D_MODEL, BF16))
IN_PAD_WIDTH = sum(s[1] for s in IN_SEGS)


def _prep_w_in(w):
    cs = np.cumsum((0,) + IN_SPLIT)

    def seg(i):
        return w[:, cs[i]:cs[i + 1]]

    pad = jnp.zeros((w.shape[0], SMALL_W - (IDX_DIM + IDX_HEADS + 2 * B_HEADS)), w.dtype)
    small = jnp.concatenate([seg(12), seg(13), seg(7), seg(8), pad], axis=1)
    return jnp.concatenate([w[:, cs[0]:cs[3]], w[:, cs[3]:cs[6]], seg(6), seg(9), seg(10), seg(11),
                            small, w[:, cs[14]:cs[17]]], axis=1).astype(BF16)


def _in_proj_kernel(x_ref, w_ref, *out_refs):
    xb = x_ref[...].astype(BF16)
    off = 0
    for (_, width, _), o_ref in zip(IN_SEGS, out_refs):
        for c0 in range(0, width, 512):
            cw = min(512, width - c0)
            o_ref[:, c0:c0 + cw] = _dot(xb, w_ref[:, off + c0:off + c0 + cw]).astype(o_ref.dtype)
        off += width


def _in_proj(x2, w_pad):
    n = x2.shape[0]
    tm = 512
    return pl.pallas_call(
        _in_proj_kernel,
        grid=(n // tm,),
        in_specs=[pl.BlockSpec((tm, D_MODEL), lambda i: (i, 0)),
                  pl.BlockSpec((D_MODEL, IN_PAD_WIDTH), lambda i: (0, 0), pipeline_mode=pl.Buffered(1))],
        out_specs=[pl.BlockSpec((tm, wd), lambda i: (i, 0)) for _, wd, _ in IN_SEGS],
        out_shape=[jax.ShapeDtypeStruct((n, wd), dt) for _, wd, dt in IN_SEGS],
        compiler_params=_cparams(("parallel",)),
        name="in_proj",
    )(x2, w_pad)


def _moba_kernel(tab_ref, q_ref, k_ref, v_ref, bias_ref, o_ref,
                 kmean_s, vt_s, qbd_s, cm_s, m_s, l_s, acc_s, *, nb):
    i = pl.program_id(1)
    BS, H, dh = MOBA_BLOCK, A_HEADS, A_HEAD_DIM

    @pl.when(i == 0)
    def _():
        def prep(n, _):
            rows = pl.ds(pl.multiple_of(n * BS, BS), BS)
            kmean_s[pl.ds(n, 1), :] = jnp.mean(k_ref[0, rows, :].astype(F32), axis=0, keepdims=True)
            vt_s[n] = v_ref[0, rows, :].astype(F32).T.astype(BF16)
            return 0

        lax.fori_loop(0, nb, prep, 0)

    qt = (q_ref[0].astype(F32) * (dh ** -0.5)).T.astype(BF16)
    zeros = jnp.zeros((dh, BS), BF16)
    for p in range(H // 2):
        top = jnp.concatenate([qt[2 * p * dh:(2 * p + 1) * dh], zeros], axis=1)
        bot = jnp.concatenate([zeros, qt[(2 * p + 1) * dh:(2 * p + 2) * dh]], axis=1)
        qbd_s[p] = jnp.concatenate([top, bot], axis=0)

    def all_heads(lhs_fn):
        return jnp.concatenate([lhs_fn(p) for p in range(H // 2)], axis=1)

    def gate_pair(p):
        km = kmean_s[:, p * LANES:(p + 1) * LANES]
        km_hi = km.astype(BF16)
        km_lo = (km - km_hi.astype(F32)).astype(BF16)
        return _dot(km_hi, qbd_s[p]) + _dot(km_lo, qbd_s[p])

    W = H * BS
    blk = lax.broadcasted_iota(jnp.int32, (nb, W), 0)
    valid = blk < i
    g = jnp.where(valid, all_heads(gate_pair), -jnp.inf)
    rank = jnp.zeros((nb, W), jnp.int32)
    for m in range(nb):
        gm = g[m:m + 1, :]
        beats = jnp.logical_or(gm > g, jnp.logical_and(gm == g, m < blk))
        rank = rank + jnp.where(beats, 1, 0)
    sel = jnp.logical_and(valid, rank < MOBA_TOPK)
    sat = jnp.concatenate([jnp.full((1, BS), tab_ref[h, N_BUCKETS - 1], F32) for h in range(H)], axis=1)
    far_bias = jnp.where(blk < i - 1, sat, 0.0)
    cm_s[...] = jnp.where(sel, far_bias, jnp.where(blk == i, 0.0, NEG))

    m_s[...] = jnp.full(m_s.shape, M_INIT, F32)
    l_s[...] = jnp.zeros(l_s.shape, F32)
    acc_s[...] = jnp.zeros(acc_s.shape, F32)

    def process(n, tile):
        rows = pl.ds(pl.multiple_of(n * BS, BS), BS)
        s = all_heads(lambda p: _dot(k_ref[0, rows, p * LANES:(p + 1) * LANES], qbd_s[p]))
        if tile is not None:
            s = s + bias_ref[tile]
        c = cm_s[pl.ds(n, 1), :]
        m_old = m_s[...]
        m_new = jnp.maximum(m_old, jnp.max(s, axis=0, keepdims=True) + c)
        pr = jnp.exp(s - (m_new - c))
        alpha = jnp.exp(m_old - m_new)
        l_s[...] = alpha * l_s[...] + jnp.sum(pr, axis=0, keepdims=True)
        m_s[...] = m_new
        pb = pr.astype(BF16)
        for h in range(H):
            hs = slice(h * dh, (h + 1) * dh)
            qs = slice(h * BS, (h + 1) * BS)
            acc_s[hs, :] = alpha[:, qs] * acc_s[hs, :] + _dot(vt_s[n, hs, :], pb[:, qs])

    def far_body(n, _):
        process(n, None)
        return 0

    lax.fori_loop(0, jnp.maximum(i - 1, 0), far_body, 0)

    @pl.when(i >= 1)
    def _():
        process(i - 1, 1)

    process(i, 0)

    inv = 1.0 / l_s[...]
    ot = jnp.concatenate([acc_s[h * dh:(h + 1) * dh, :] * inv[:, h * BS:(h + 1) * BS] for h in range(H)],
                         axis=0)
    o_ref[0] = ot.T.astype(o_ref.dtype)


def _moba(qkv, tab, bias):
    B, T, _ = qkv.shape
    BS = MOBA_BLOCK
    nb = T // BS
    W = A_HEADS * BS
    return pl.pallas_call(
        functools.partial(_moba_kernel, nb=nb),
        grid=(B, nb),
        in_specs=[pl.BlockSpec(memory_space=pltpu.SMEM),
                  pl.BlockSpec((1, BS, A_WIDTH), lambda b, i: (b, i, 0)),
                  pl.BlockSpec((1, T, A_WIDTH), lambda b, i: (b, 0, 1)),
                  pl.BlockSpec((1, T, A_WIDTH), lambda b, i: (b, 0, 2)),
                  pl.BlockSpec(bias.shape, lambda b, i: (0, 0, 0))],
        out_specs=pl.BlockSpec((1, BS, A_WIDTH), lambda b, i: (b, i, 0)),
        out_shape=jax.ShapeDtypeStruct((B, T, A_WIDTH), BF16),
        scratch_shapes=[pltpu.VMEM((nb, A_WIDTH), F32),
                        pltpu.VMEM((nb, A_WIDTH, BS), BF16),
                        pltpu.VMEM((A_HEADS // 2, 2 * A_HEAD_DIM, 2 * BS), BF16),
                        pltpu.VMEM((nb, W), F32),
                        pltpu.VMEM((1, W), F32),
                        pltpu.VMEM((1, W), F32),
                        pltpu.VMEM((A_WIDTH, BS), F32)],
        compiler_params=_cparams(("parallel", "arbitrary")),
        name="moba",
    )(tab, qkv, qkv, qkv, bias)


def _softplus(x):
    return jnp.maximum(x, 0.0) + jnp.log(1.0 + jnp.exp(-jnp.abs(x)))


def _gdn_kernel(x_ref, z_ref, ab_ref, abt_ref, cw_ref, alog_ref, dtb_ref, alogt_ref, dtbt_ref, gn_ref,
                o_ref, xpad_s, state_s, *, tt_rows):
    tt = pl.program_id(1)
    C = GDN_CHUNK
    dk = B_HEAD_DIM
    TT = tt_rows

    @pl.when(tt == 0)
    def _():
        state_s[...] = jnp.zeros_like(state_s)
        xpad_s[0:8, :] = jnp.zeros((8, 3 * B_WIDTH), F32)

    xpad_s[8:8 + TT, :] = x_ref[0]

    ab = ab_ref[0]
    abt = abt_ref[0]
    la_cols = -jnp.exp(alog_ref[...]) * _softplus(ab[:, 0:B_HEADS] + dtb_ref[...])
    be_cols = _sigmoid(ab[:, B_HEADS:2 * B_HEADS])
    la_rows = -jnp.exp(alogt_ref[...]) * _softplus(abt[0:B_HEADS, :] + dtbt_ref[...])

    ri = lax.broadcasted_iota(jnp.int32, (C, C), 0)
    ci = lax.broadcasted_iota(jnp.int32, (C, C), 1)
    causal = ci <= ri
    strict = ci < ri
    low = jnp.where(causal, 1.0, 0.0)
    upp = jnp.where(ri <= ci, 1.0, 0.0)
    eye = jnp.where(ci == ri, 1.0, 0.0)
    zpad = jnp.zeros((dk - C, dk), F32)

    def conv_silu(col0):
        acc = jnp.zeros((TT, dk), F32)
        for j in range(CONV_WIDTH):
            r0 = 8 - (CONV_WIDTH - 1) + j
            acc = acc + xpad_s[r0:r0 + TT, col0:col0 + dk] * cw_ref[j:j + 1, col0:col0 + dk]
        return acc * _sigmoid(acc)

    for h in range(B_HEADS):
        qh = conv_silu(h * dk)
        kh = conv_silu(B_WIDTH + h * dk)
        vh = conv_silu(2 * B_WIDTH + h * dk)
        qh = qh * lax.rsqrt(jnp.sum(qh * qh, axis=-1, keepdims=True) + RMS_EPS) * (dk ** -0.5)
        kh = kh * lax.rsqrt(jnp.sum(kh * kh, axis=-1, keepdims=True) + RMS_EPS)
        for c in range(TT // C):
            r0 = c * C
            q = qh[r0:r0 + C]
            k = kh[r0:r0 + C]
            v = vh[r0:r0 + C]
            la_r = la_rows[h:h + 1, r0:r0 + C]
            la_c = la_cols[r0:r0 + C, h:h + 1]
            be = be_cols[r0:r0 + C, h:h + 1]
            g_col = jnp.sum(low * la_r, axis=1, keepdims=True)
            g_row = jnp.sum(upp * la_c, axis=0, keepdims=True)
            diff = g_col - g_row
            decay = jnp.where(causal, jnp.exp(jnp.where(causal, diff, 0.0)), 0.0)
            kb = k * be
            mm = jnp.where(strict, -_hdot_nt(kb, k) * decay, 0.0)
            tinv = eye + mm
            pw = mm
            for _ in range(int(math.log2(C)) - 1):
                pw = _hdot(pw, pw)
                tinv = tinv + _hdot(tinv, pw)
            eg = jnp.exp(g_col)
            u = _hdot(tinv, v * be)
            w = _hdot(tinv, kb * eg)
            a_intra = _hdot_nt(q, k) * decay
            S = state_s[h]
            v_new = u - _hdot(w, S)
            o = _hdot(q * eg, S) + _hdot(a_intra, v_new)
            g_last = g_col[C - 1:C, :]
            kd = k * jnp.exp(g_last - g_col)
            kdt = jnp.concatenate([kd, zpad], axis=0).T
            state_s[h] = S * jnp.exp(g_last) + _hdot(kdt, jnp.concatenate([v_new, zpad], axis=0))
            z = z_ref[0, r0:r0 + C, h * dk:(h + 1) * dk]
            on = o * lax.rsqrt(jnp.mean(o * o, axis=-1, keepdims=True) + RMS_EPS) * gn_ref[...]
            o_ref[0, r0:r0 + C, h * dk:(h + 1) * dk] = (on * (z * _sigmoid(z))).astype(o_ref.dtype)

    xpad_s[0:8, :] = xpad_s[TT:TT + 8, :]


def _gdn(xqkv, z, ab, abt, conv_w, a_log, dt_bias, gdn_norm):
    B, T, _ = xqkv.shape
    TT = 256
    full = lambda shape: pl.BlockSpec(shape, lambda b, t: (0,) * len(shape))
    return pl.pallas_call(
        functools.partial(_gdn_kernel, tt_rows=TT),
        grid=(B, T // TT),
        in_specs=[pl.BlockSpec((1, TT, 3 * B_WIDTH), lambda b, t: (b, t, 0)),
                  pl.BlockSpec((1, TT, B_WIDTH), lambda b, t: (b, t, 0)),
                  pl.BlockSpec((1, TT, 2 * B_HEADS), lambda b, t: (b, t, 0)),
                  pl.BlockSpec((1, 2 * B_HEADS, TT), lambda b, t: (b, 0, t)),
                  full((CONV_WIDTH, 3 * B_WIDTH)),
                  full((1, B_HEADS)), full((1, B_HEADS)), full((B_HEADS, 1)), full((B_HEADS, 1)),
                  full((1, B_HEAD_DIM))],
        out_specs=pl.BlockSpec((1, TT, B_WIDTH), lambda b, t: (b, t, 0)),
        out_shape=jax.ShapeDtypeStruct((B, T, B_WIDTH), BF16),
        scratch_shapes=[pltpu.VMEM((TT + 8, 3 * B_WIDTH), F32),
                        pltpu.VMEM((B_HEADS, B_HEAD_DIM, B_HEAD_DIM), F32)],
        compiler_params=_cparams(("parallel", "arbitrary")),
        name="gdn",
    )(xqkv, z, ab, abt, conv_w, a_log.reshape(1, -1), dt_bias.reshape(1, -1),
      a_log.reshape(-1, 1), dt_bias.reshape(-1, 1), gdn_norm.reshape(1, -1))


def _dsa_prep_kernel(ckv_ref, small_ref, kvn_ref, lng_ref, lnb_ref, cn_ref, kin_ref):
    c = ckv_ref[...]
    ms = jnp.mean(c * c, axis=-1, keepdims=True)
    cn_ref[...] = (c * lax.rsqrt(ms + RMS_EPS) * kvn_ref[...]).astype(cn_ref.dtype)
    ki = small_ref[:, SMALL_KI:SMALL_KI + IDX_DIM]
    kin_ref[...] = _layer_norm(ki, lng_ref[...], lnb_ref[...]).astype(kin_ref.dtype)


def _dsa_prep(ckv, small, kv_norm, ln_g, ln_b):
    n = ckv.shape[0]
    tm = 1024
    return pl.pallas_call(
        _dsa_prep_kernel,
        grid=(n // tm,),
        in_specs=[pl.BlockSpec((tm, KV_RANK), lambda i: (i, 0)),
                  pl.BlockSpec((tm, SMALL_W), lambda i: (i, 0)),
                  pl.BlockSpec((1, KV_RANK), lambda i: (0, 0)),
                  pl.BlockSpec((1, IDX_DIM), lambda i: (0, 0)),
                  pl.BlockSpec((1, IDX_DIM), lambda i: (0, 0))],
        out_specs=[pl.BlockSpec((tm, KV_RANK), lambda i: (i, 0)),
                   pl.BlockSpec((tm, IDX_DIM), lambda i: (i, 0))],
        out_shape=[jax.ShapeDtypeStruct((n, KV_RANK), BF16), jax.ShapeDtypeStruct((n, IDX_DIM), BF16)],
        compiler_params=_cparams(("parallel",)),
        name="dsa_prep",
    )(ckv, small, kv_norm.reshape(1, -1), ln_g.reshape(1, -1), ln_b.reshape(1, -1))


def _key_to_f32(k):
    return lax.bitcast_convert_type(jnp.where(k >= 0, k, k ^ 0x7FFFFFFF), F32)


def _dsa_kernel(qc_ref, qi_ref, wi_ref, cn_ref, kin_ref, wuk_ref, wuv_ref, bias_ref, o_ref,
                sc_s, acc_s, *, topk):
    j = pl.program_id(1)
    QB, KW, H, R = DSA_QB, DSA_KW, C_HEADS, KV_RANK
    nck = (j * QB + QB - 1) // KW + 1

    qlat = _dot(qc_ref[0], wuk_ref[...])
    qs = jnp.concatenate([qlat[:, h * R:(h + 1) * R] for h in range(H)], axis=0).astype(BF16)

    qi = qi_ref[0, 0]
    wcol = wi_ref[0, 0] * IDX_WEIGHT_SCALE
    tq = j * QB + lax.broadcasted_iota(jnp.int32, (QB, KW), 0)
    tk0 = lax.broadcasted_iota(jnp.int32, (QB, KW), 1)

    def idx_body(c, _):
        rows = pl.ds(pl.multiple_of(c * KW, KW), KW)
        s8 = jnp.maximum(_dot_nt(qi, kin_ref[0, rows, :]), 0.0) * wcol
        sc = s8[0:QB]
        for h in range(1, IDX_HEADS):
            sc = sc + s8[h * QB:(h + 1) * QB]
        sc_s[c] = jnp.where(tk0 + c * KW <= tq, sc, -jnp.inf)
        return 0

    lax.fori_loop(0, nck, idx_body, 0)

    trow = j * QB + lax.broadcasted_iota(jnp.int32, (QB, 1), 0)
    kk = jnp.minimum(trow + 1, topk)

    def bis_body(_, carry):
        lo, hi, cge, cgt = carry
        mid = (lo >> 1) + (hi >> 1) + (lo & hi & 1) + ((lo ^ hi) & 1)
        midb = jnp.broadcast_to(_key_to_f32(mid), (QB, KW))

        def cnt_body(c, acc):
            return acc + jnp.where(sc_s[c] >= midb, 1, 0)

        acc = lax.fori_loop(0, nck, cnt_body, jnp.zeros((QB, KW), jnp.int32))
        cnt = jnp.sum(acc, axis=-1, keepdims=True)
        ok = cnt >= kk
        return (jnp.where(ok, mid, lo), jnp.where(ok, hi, mid - 1),
                jnp.where(ok, cnt, cge), jnp.where(ok, cgt, cnt))

    init = (jnp.full((QB, 1), KEY_LO, jnp.int32), jnp.full((QB, 1), KEY_HI, jnp.int32),
            trow + 1, jnp.zeros((QB, 1), jnp.int32))
    lo, _, cge, cgt = lax.fori_loop(0, 32, bis_body, init)
    taub = jnp.broadcast_to(_key_to_f32(lo), (QB, KW))

    @pl.when(jnp.max(cge - kk) > 0)
    def _():
        need = (kk - cgt).astype(F32)
        ur = lax.broadcasted_iota(jnp.int32, (KW, KW), 0)
        uc = lax.broadcasted_iota(jnp.int32, (KW, KW), 1)
        upper = jnp.where(ur <= uc, 1.0, 0.0).astype(BF16)

        def tie_body(c, seen):
            s = sc_s[c]
            eq = s == taub
            pc = _dot(jnp.where(eq, 1.0, 0.0).astype(BF16), upper) + seen
            sc_s[c] = jnp.where(jnp.logical_and(eq, pc > need), -jnp.inf, s)
            return pc[:, KW - 1:KW]

        lax.fori_loop(0, nck, tie_body, jnp.zeros((QB, 1), F32))

    acc_s[...] = jnp.zeros_like(acc_s)

    def att_body(c, carry):
        m, l = carry
        rows = pl.ds(pl.multiple_of(c * KW, KW), KW)
        cc = cn_ref[0, rows, :]
        madd = jnp.where(sc_s[c] >= taub, 0.0, NEG)
        bias = bias_ref[jnp.minimum((j * QB - c * KW) // QB, 3)]
        s = _dot_nt(qs, cc).reshape(H, QB, KW) + (bias.reshape(H, QB, KW) + madd[None])
        s = s.reshape(H * QB, KW)
        m_new = jnp.maximum(m, jnp.max(s, axis=-1, keepdims=True))
        alpha = jnp.exp(m - m_new)
        p = jnp.exp(s - m_new)
        l = alpha * l + jnp.sum(p, axis=-1, keepdims=True)
        acc_s[...] = alpha * acc_s[...] + _dot(p.astype(BF16), cc)
        return m_new, l

    init = (jnp.full((H * QB, 1), -1e30, F32), jnp.zeros((H * QB, 1), F32))
    _, l = lax.fori_loop(0, nck, att_body, init)
    o_lat = (acc_s[...] / l).astype(BF16)
    o2 = jnp.concatenate([o_lat[h * QB:(h + 1) * QB] for h in range(H)], axis=1)
    o_ref[0] = _dot(o2, wuv_ref[...]).astype(o_ref.dtype)


def _dsa(q_c, qi_st, wi_st, cn, kin, wuk_bd, wuv_bd, bias):
    B, T, _ = q_c.shape
    QB, KW = DSA_QB, DSA_KW
    nq = T // QB
    topk = min(DSA_TOPK_MAX, T // 4)
    return pl.pallas_call(
        functools.partial(_dsa_kernel, topk=topk),
        grid=(B, nq),
        in_specs=[pl.BlockSpec((1, QB, C_WIDTH), lambda b, j: (b, j, 0)),
                  pl.BlockSpec((1, 1, C_HEADS * QB, IDX_DIM), lambda b, j: (b, j, 0, 0)),
                  pl.BlockSpec((1, 1, C_HEADS * QB, 1), lambda b, j: (b, j, 0, 0)),
                  pl.BlockSpec((1, T, KV_RANK), lambda b, j: (b, 0, 0)),
                  pl.BlockSpec((1, T, IDX_DIM), lambda b, j: (b, 0, 0)),
                  pl.BlockSpec(wuk_bd.shape, lambda b, j: (0, 0)),
                  pl.BlockSpec(wuv_bd.shape, lambda b, j: (0, 0)),
                  pl.BlockSpec(bias.shape, lambda b, j: (0, 0, 0))],
        out_specs=pl.BlockSpec((1, QB, C_WIDTH), lambda b, j: (b, j, 0)),
        out_shape=jax.ShapeDtypeStruct((B, T, C_WIDTH), BF16),
        scratch_shapes=[pltpu.VMEM((T // KW, QB, KW), F32),
                        pltpu.VMEM((C_HEADS * QB, KV_RANK), F32)],
        compiler_params=_cparams(("parallel", "arbitrary")),
        name="dsa",
    )(q_c, qi_st, wi_st, cn, kin, wuk_bd, wuv_bd, bias)


def _merge_kernel(x_ref, oa_ref, ob_ref, oc_ref, g_ref, wa_ref, wb_ref, wc_ref, wo_ref, lg_ref, lb_ref,
                  out_ref):
    D = D_MODEL
    y = _sigmoid(g_ref[:, 0:D].astype(F32)) * _dot(oa_ref[...], wa_ref[...])
    y = y + _sigmoid(g_ref[:, D:2 * D].astype(F32)) * _dot(ob_ref[...], wb_ref[...])
    y = y + _sigmoid(g_ref[:, 2 * D:3 * D].astype(F32)) * _dot(oc_ref[...], wc_ref[...])
    mix = _dot(y.astype(BF16), wo_ref[...])
    out_ref[...] = _layer_norm(DEEPNORM_ALPHA * x_ref[...] + mix, lg_ref[...], lb_ref[...])


def _merge(x2, oa, ob, oc, gates, wa, wb, wc, wo, ln_g, ln_b):
    n = x2.shape[0]
    tm = 512
    row = lambda w: pl.BlockSpec((tm, w), lambda i: (i, 0))
    full = lambda a: pl.BlockSpec(a.shape, lambda i: (0, 0))
    lg, lb = ln_g.reshape(1, -1), ln_b.reshape(1, -1)
    return pl.pallas_call(
        _merge_kernel,
        grid=(n // tm,),
        in_specs=[row(D_MODEL), row(A_WIDTH), row(B_WIDTH), row(C_WIDTH), row(3 * D_MODEL),
                  full(wa), full(wb), full(wc), full(wo), full(lg), full(lb)],
        out_specs=row(D_MODEL),
        out_shape=jax.ShapeDtypeStruct((n, D_MODEL), F32),
        compiler_params=_cparams(("parallel",)),
        name="merge",
    )(x2, oa, ob, oc, gates, wa, wb, wc, wo, lg, lb)


def _mlp_kernel(x_ref, wu_ref, wd_ref, lg_ref, lb_ref, out_ref):
    x = x_ref[...]
    xb = x.astype(BF16)
    acc = jnp.zeros(x.shape, F32)
    fc = 1024
    for c in range(D_FF // fc):
        h = jnp.maximum(_dot(xb, wu_ref[:, c * fc:(c + 1) * fc]), 0.0)
        acc = acc + _dot((h * h).astype(BF16), wd_ref[c * fc:(c + 1) * fc, :])
    out_ref[...] = _layer_norm(DEEPNORM_ALPHA * x + acc, lg_ref[...], lb_ref[...])


def _mlp(x2, wu, wd, ln_g, ln_b):
    n = x2.shape[0]
    tm = 512
    lg, lb = ln_g.reshape(1, -1), ln_b.reshape(1, -1)
    return pl.pallas_call(
        _mlp_kernel,
        grid=(n // tm,),
        in_specs=[pl.BlockSpec((tm, D_MODEL), lambda i: (i, 0)),
                  pl.BlockSpec(wu.shape, lambda i: (0, 0), pipeline_mode=pl.Buffered(1)),
                  pl.BlockSpec(wd.shape, lambda i: (0, 0), pipeline_mode=pl.Buffered(1)),
                  pl.BlockSpec(lg.shape, lambda i: (0, 0)),
                  pl.BlockSpec(lb.shape, lambda i: (0, 0))],
        out_specs=pl.BlockSpec((tm, D_MODEL), lambda i: (i, 0)),
        out_shape=jax.ShapeDtypeStruct((n, D_MODEL), F32),
        compiler_params=_cparams(("parallel",)),
        name="mlp",
    )(x2, wu, wd, lg, lb)


def _block_diag_uk(w_uk):
    R, H, dh = w_uk.shape
    eye = jnp.eye(H, dtype=w_uk.dtype)
    bd = jnp.einsum("rhd,hg->hdgr", w_uk * (dh ** -0.5), eye)
    return bd.reshape(H * dh, H * R).astype(BF16)


def _block_diag_uv(w_uv):
    R, H, dh = w_uv.shape
    eye = jnp.eye(H, dtype=w_uv.dtype)
    bd = jnp.einsum("rhd,hg->hrgd", w_uv, eye)
    return bd.reshape(H * R, H * dh).astype(BF16)


def _mixer_branches(x2, B, T, w_pad, conv_w, a_log, dt_bias, gdn_norm, kv_norm, ln_g, ln_b,
                    wuk_bd, wuv_bd, tab, bias_a, bias_c):
    qkv_a, qkv_b, z_b, q_c, ckv, qi, small, gates = _in_proj(x2, w_pad)

    o_a = _moba(qkv_a.reshape(B, T, -1), tab, bias_a).reshape(B * T, A_WIDTH)

    ab = small[:, SMALL_A:SMALL_A + 2 * B_HEADS].reshape(B, T, 2 * B_HEADS)
    o_b = _gdn(qkv_b.reshape(B, T, -1), z_b.reshape(B, T, -1), ab, ab.transpose(0, 2, 1),
               conv_w, a_log, dt_bias, gdn_norm).reshape(B * T, B_WIDTH)

    nq = T // DSA_QB
    cn, kin = _dsa_prep(ckv, small, kv_norm, ln_g, ln_b)
    qi_st = (qi.reshape(B, nq, DSA_QB, IDX_HEADS, IDX_DIM).transpose(0, 1, 3, 2, 4)
             .reshape(B, nq, IDX_HEADS * DSA_QB, IDX_DIM))
    wi_st = (small[:, SMALL_WI:SMALL_WI + IDX_HEADS].reshape(B, nq, DSA_QB, IDX_HEADS)
             .transpose(0, 1, 3, 2).reshape(B, nq, IDX_HEADS * DSA_QB, 1))
    o_c = _dsa(q_c.reshape(B, T, -1), qi_st, wi_st, cn.reshape(B, T, -1), kin.reshape(B, T, -1),
               wuk_bd, wuv_bd, bias_c).reshape(B * T, C_WIDTH)
    return o_a, o_b, o_c, gates


def kernel(x, rel_bias, w_in, conv_w, a_log, dt_bias, gdn_norm, kv_norm, idx_k_ln_g, idx_k_ln_b, w_uk, w_uv, w_branch_a, w_branch_b, w_branch_c, w_out, ln1_g, ln1_b, w_up, w_down, ln2_g, ln2_b):
    B, T, D = x.shape
    bias_a, bias_c = _bias_tiles(rel_bias)
    tab = rel_bias.T
    x2 = x.reshape(B * T, D)
    for i in range(w_in.shape[0]):
        o_a, o_b, o_c, gates = _mixer_branches(
            x2, B, T, _prep_w_in(w_in[i]), conv_w[i], a_log[i], dt_bias[i], gdn_norm[i], kv_norm[i],
            idx_k_ln_g[i], idx_k_ln_b[i], _block_diag_uk(w_uk[i]), _block_diag_uv(w_uv[i]),
            tab, bias_a, bias_c)
        x2 = _merge(x2, o_a, o_b, o_c, gates, w_branch_a[i].astype(BF16), w_branch_b[i].astype(BF16),
                    w_branch_c[i].astype(BF16), w_out[i].astype(BF16), ln1_g[i], ln1_b[i])
        x2 = _mlp(x2, w_up[i].astype(BF16), w_down[i].astype(BF16), ln2_g[i], ln2_b[i])
    return x2.reshape(B, T, D)
```

```python
import functools
import math

import numpy as np
import jax
import jax.numpy as jnp
from jax import lax
from jax.experimental import pallas as pl
from jax.experimental.pallas import tpu as pltpu

D_MODEL = 1024
DEPTH = 4
A_HEADS = 8
A_HEAD_DIM = 64
A_WIDTH = A_HEADS * A_HEAD_DIM
MOBA_BLOCK = 256
MOBA_TOPK = 3
B_HEADS = 4
B_HEAD_DIM = 128
B_WIDTH = B_HEADS * B_HEAD_DIM
CONV_WIDTH = 4
GDN_CHUNK = 64
C_HEADS = 8
C_HEAD_DIM = 64
C_WIDTH = C_HEADS * C_HEAD_DIM
KV_RANK = 256
IDX_HEADS = 8
IDX_DIM = 64
DSA_TOPK_MAX = 256
IDX_WEIGHT_SCALE = IDX_HEADS ** -0.5 * IDX_DIM ** -0.5
N_BUCKETS = 32
MAX_DISTANCE = 128
D_FF = 4 * D_MODEL
DEEPNORM_ALPHA = (2 * DEPTH) ** 0.25
LN_EPS = 1e-5
RMS_EPS = 1e-6

IN_SPLIT = (A_WIDTH, A_WIDTH, A_WIDTH,
            B_WIDTH, B_WIDTH, B_WIDTH, B_WIDTH, B_HEADS, B_HEADS,
            C_WIDTH, KV_RANK, IDX_HEADS * IDX_DIM, IDX_DIM, IDX_HEADS,
            D_MODEL, D_MODEL, D_MODEL)

F32 = jnp.float32
BF16 = jnp.bfloat16
NEG = -1e9
M_INIT = -1e8
LANES = 128
DSA_QB = 128
DSA_KW = 256
DSA_GROUPS = 1
DSA_CNT_UNROLL = 4
SMALL_W = 128
SMALL_KI = 0
SMALL_WI = IDX_DIM
SMALL_A = SMALL_WI + IDX_HEADS
SMALL_B = SMALL_A + B_HEADS
VMEM_LIMIT = 56 * 1024 * 1024

KEY_LO = int(np.int32(np.uint32(0xFF7FFFFF) ^ np.uint32(0x7FFFFFFF)))
KEY_HI = 0x7F800000


def _cparams(sem):
    return pltpu.CompilerParams(dimension_semantics=sem, vmem_limit_bytes=VMEM_LIMIT)


def _dot(a, b):
    return jnp.dot(a, b, preferred_element_type=F32)


def _dot_nt(a, b):
    return lax.dot_general(a, b, (((1,), (1,)), ((), ())), preferred_element_type=F32)


def _split(a):
    hi = a.astype(BF16)
    return hi, (a - hi.astype(F32)).astype(BF16)


def _bmm3(a, b, nt=False):
    dims = (((2,), (2 if nt else 1,)), ((0,), (0,)))
    mm = lambda x, y: lax.dot_general(x, y, dims, preferred_element_type=F32)
    return mm(a[0], b[0]) + (mm(a[0], b[1]) + mm(a[1], b[0]))


def _layer_norm(r, g, b):
    mu = jnp.mean(r, axis=-1, keepdims=True)
    d = r - mu
    var = jnp.mean(d * d, axis=-1, keepdims=True)
    return d * lax.rsqrt(var + LN_EPS) * g + b


def _sigmoid(x):
    return 1.0 / (1.0 + jnp.exp(-x))


def _t5_bucket_np(d):
    max_exact = N_BUCKETS // 2
    d = np.maximum(d, 0)
    large = max_exact + (np.log(np.maximum(d, max_exact).astype(np.float32) / max_exact)
                         / math.log(MAX_DISTANCE / max_exact) * (N_BUCKETS - max_exact)).astype(np.int32)
    return np.where(d < max_exact, d, np.minimum(large, N_BUCKETS - 1))


def _bucket_tiles(qb, w, offs):
    tq = np.arange(qb)[:, None]
    tk = np.arange(w)[None, :]
    tiles = []
    for off in offs:
        d = off + tq - tk
        tiles.append(np.where(d >= 0, _t5_bucket_np(d), -1))
    return np.stack(tiles).astype(np.int32)


def _bias_kernel(tab_ref, mi_ref, di_ref, mo_ref, do_ref):
    h = pl.program_id(0)

    def build(idx, col):
        acc = jnp.full(idx.shape, NEG, F32)
        for b in range(N_BUCKETS):
            acc = jnp.where(idx == b, tab_ref[col, b], acc)
        return acc

    mo_ref[...] = build(mi_ref[...], h)
    do_ref[...] = build(di_ref[...], A_HEADS + h)


def _bias_tiles(rel_bias):
    mi = jnp.asarray(_bucket_tiles(MOBA_BLOCK, MOBA_BLOCK, (0, MOBA_BLOCK)).transpose(0, 2, 1))
    di = jnp.asarray(_bucket_tiles(DSA_QB, DSA_KW, (0, DSA_QB, 2 * DSA_QB)).transpose(0, 2, 1))
    return pl.pallas_call(
        _bias_kernel,
        grid=(A_HEADS,),
        in_specs=[pl.BlockSpec(memory_space=pltpu.SMEM),
                  pl.BlockSpec(mi.shape, lambda h: (0, 0, 0)),
                  pl.BlockSpec(di.shape, lambda h: (0, 0, 0))],
        out_specs=[pl.BlockSpec(mi.shape, lambda h: (0, 0, h)),
                   pl.BlockSpec(di.shape, lambda h: (0, 0, h))],
        out_shape=[jax.ShapeDtypeStruct(mi.shape[:2] + (A_HEADS * mi.shape[2],), F32),
                   jax.ShapeDtypeStruct(di.shape[:2] + (C_HEADS * di.shape[2],), F32)],
        compiler_params=_cparams(("arbitrary",)),
        name="bias_tiles",
    )(rel_bias.T, mi, di)


IN_SEGS = (("qkv_a", 3 * A_WIDTH, BF16), ("qkv_b", 3 * B_WIDTH, F32), ("z_b", B_WIDTH, F32),
           ("q_c", C_WIDTH, BF16), ("ckv", KV_RANK, F32), ("qi", IDX_HEADS * IDX_DIM, BF16),
           ("small", SMALL_W, F32), ("gates", 3 * D_MODEL, BF16))
IN_PAD_WIDTH = sum(s[1] for s in IN_SEGS)


def _prep_w_in(w):
    cs = np.cumsum((0,) + IN_SPLIT)

    def seg(i):
        return w[:, cs[i]:cs[i + 1]]

    pad = jnp.zeros((w.shape[0], SMALL_W - (IDX_DIM + IDX_HEADS + 2 * B_HEADS)), w.dtype)
    small = jnp.concatenate([seg(12), seg(13), seg(7), seg(8), pad], axis=1)
    return jnp.concatenate([w[:, cs[0]:cs[3]], w[:, cs[3]:cs[6]], seg(6), seg(9), seg(10), seg(11),
                            small, w[:, cs[14]:cs[17]]], axis=1).astype(BF16)


def _in_proj_kernel(x_ref, w_ref, *out_refs):
    xb = x_ref[...].astype(BF16)
    off = 0
    for (_, width, _), o_ref in zip(IN_SEGS, out_refs):
        for c0 in range(0, width, 512):
            cw = min(512, width - c0)
            o_ref[:, c0:c0 + cw] = _dot(xb, w_ref[:, off + c0:off + c0 + cw]).astype(o_ref.dtype)
        off += width


def _in_proj(x2, w_pad):
    n = x2.shape[0]
    tm = 512
    return pl.pallas_call(
        _in_proj_kernel,
        grid=(n // tm,),
        in_specs=[pl.BlockSpec((tm, D_MODEL), lambda i: (i, 0)),
                  pl.BlockSpec((D_MODEL, IN_PAD_WIDTH), lambda i: (0, 0), pipeline_mode=pl.Buffered(1))],
        out_specs=[pl.BlockSpec((tm, wd), lambda i: (i, 0)) for _, wd, _ in IN_SEGS],
        out_shape=[jax.ShapeDtypeStruct((n, wd), dt) for _, wd, dt in IN_SEGS],
        compiler_params=_cparams(("parallel",)),
        name="in_proj",
    )(x2, w_pad)


def _moba_kernel(tab_ref, q_ref, k_ref, v_ref, bias_ref, o_ref,
                 kmean_s, vt_s, qbd_s, cm_s, m_s, l_s, acc_s, *, nb):
    i = pl.program_id(1)
    BS, H, dh = MOBA_BLOCK, A_HEADS, A_HEAD_DIM

    @pl.when(i == 0)
    def _():
        def prep(n, _):
            rows = pl.ds(pl.multiple_of(n * BS, BS), BS)
            kmean_s[pl.ds(n, 1), :] = jnp.mean(k_ref[0, rows, :].astype(F32), axis=0, keepdims=True)
            vt_s[n] = v_ref[0, rows, :].astype(F32).T.astype(BF16)
            return 0

        lax.fori_loop(0, nb, prep, 0)

    qt = (q_ref[0].astype(F32) * (dh ** -0.5)).T.astype(BF16)
    zeros = jnp.zeros((dh, BS), BF16)
    for p in range(H // 2):
        top = jnp.concatenate([qt[2 * p * dh:(2 * p + 1) * dh], zeros], axis=1)
        bot = jnp.concatenate([zeros, qt[(2 * p + 1) * dh:(2 * p + 2) * dh]], axis=1)
        qbd_s[p] = jnp.concatenate([top, bot], axis=0)

    def all_heads(lhs_fn):
        return jnp.concatenate([lhs_fn(p) for p in range(H // 2)], axis=1)

    def gate_pair(p):
        km = kmean_s[:, p * LANES:(p + 1) * LANES]
        km_hi = km.astype(BF16)
        km_lo = (km - km_hi.astype(F32)).astype(BF16)
        return _dot(km_hi, qbd_s[p]) + _dot(km_lo, qbd_s[p])

    W = H * BS
    blk = lax.broadcasted_iota(jnp.int32, (nb, W), 0)
    valid = blk < i
    g = jnp.where(valid, all_heads(gate_pair), -jnp.inf)
    rank = jnp.zeros((nb, W), jnp.int32)
    for m in range(nb):
        gm = g[m:m + 1, :]
        beats = jnp.logical_or(gm > g, jnp.logical_and(gm == g, m < blk))
        rank = rank + jnp.where(beats, 1, 0)
    sel = jnp.logical_and(valid, rank < MOBA_TOPK)
    sat = jnp.concatenate([jnp.full((1, BS), tab_ref[h, N_BUCKETS - 1], F32) for h in range(H)], axis=1)
    far_bias = jnp.where(blk < i - 1, sat, 0.0)
    cm_s[...] = jnp.where(sel, far_bias, jnp.where(blk == i, 0.0, NEG))

    m_s[...] = jnp.full(m_s.shape, M_INIT, F32)
    l_s[...] = jnp.zeros(l_s.shape, F32)
    acc_s[...] = jnp.zeros(acc_s.shape, F32)

    def process(n, tile):
        rows = pl.ds(pl.multiple_of(n * BS, BS), BS)
        s = all_heads(lambda p: _dot(k_ref[0, rows, p * LANES:(p + 1) * LANES], qbd_s[p]))
        if tile is not None:
            s = s + bias_ref[tile]
        c = cm_s[pl.ds(n, 1), :]
        m_old = m_s[...]
        m_new = jnp.maximum(m_old, jnp.max(s, axis=0, keepdims=True) + c)
        pr = jnp.exp(s - (m_new - c))
        alpha = jnp.exp(m_old - m_new)
        l_s[...] = alpha * l_s[...] + jnp.sum(pr, axis=0, keepdims=True)
        m_s[...] = m_new
        pb = pr.astype(BF16)
        for h in range(H):
            hs = slice(h * dh, (h + 1) * dh)
            qs = slice(h * BS, (h + 1) * BS)
            acc_s[hs, :] = alpha[:, qs] * acc_s[hs, :] + _dot(vt_s[n, hs, :], pb[:, qs])

    def far_body(n, _):
        process(n, None)
        return 0

    lax.fori_loop(0, jnp.maximum(i - 1, 0), far_body, 0)

    @pl.when(i >= 1)
    def _():
        process(i - 1, 1)

    process(i, 0)

    inv = 1.0 / l_s[...]
    ot = jnp.concatenate([acc_s[h * dh:(h + 1) * dh, :] * inv[:, h * BS:(h + 1) * BS] for h in range(H)],
                         axis=0)
    o_ref[0] = ot.T.astype(o_ref.dtype)


def _moba(qkv, tab, bias):
    B, T, _ = qkv.shape
    BS = MOBA_BLOCK
    nb = T // BS
    W = A_HEADS * BS
    return pl.pallas_call(
        functools.partial(_moba_kernel, nb=nb),
        grid=(B, nb),
        in_specs=[pl.BlockSpec(memory_space=pltpu.SMEM),
                  pl.BlockSpec((1, BS, A_WIDTH), lambda b, i: (b, i, 0)),
                  pl.BlockSpec((1, T, A_WIDTH), lambda b, i: (b, 0, 1)),
                  pl.BlockSpec((1, T, A_WIDTH), lambda b, i: (b, 0, 2)),
                  pl.BlockSpec(bias.shape, lambda b, i: (0, 0, 0))],
        out_specs=pl.BlockSpec((1, BS, A_WIDTH), lambda b, i: (b, i, 0)),
        out_shape=jax.ShapeDtypeStruct((B, T, A_WIDTH), BF16),
        scratch_shapes=[pltpu.VMEM((nb, A_WIDTH), F32),
                        pltpu.VMEM((nb, A_WIDTH, BS), BF16),
                        pltpu.VMEM((A_HEADS // 2, 2 * A_HEAD_DIM, 2 * BS), BF16),
                        pltpu.VMEM((nb, W), F32),
                        pltpu.VMEM((1, W), F32),
                        pltpu.VMEM((1, W), F32),
                        pltpu.VMEM((A_WIDTH, BS), F32)],
        compiler_params=_cparams(("parallel", "arbitrary")),
        name="moba",
    )(tab, qkv, qkv, qkv, bias)


def _softplus(x):
    return jnp.maximum(x, 0.0) + jnp.log(1.0 + jnp.exp(-jnp.abs(x)))


def _gdn_kernel(x_ref, z_ref, ab_ref, abt_ref, cw_ref, alog_ref, dtb_ref, alogt_ref, dtbt_ref, gn_ref,
                o_ref, xpad_s, state_s, *, tt_rows):
    tt = pl.program_id(1)
    C = GDN_CHUNK
    dk = B_HEAD_DIM
    TT = tt_rows

    @pl.when(tt == 0)
    def _():
        state_s[...] = jnp.zeros_like(state_s)
        xpad_s[0:8, :] = jnp.zeros((8, 3 * B_WIDTH), F32)

    xpad_s[8:8 + TT, :] = x_ref[0]

    ab = ab_ref[0]
    abt = abt_ref[0]
    la_cols = -jnp.exp(alog_ref[...]) * _softplus(ab[:, 0:B_HEADS] + dtb_ref[...])
    be_cols = _sigmoid(ab[:, B_HEADS:2 * B_HEADS])
    la_rows = -jnp.exp(alogt_ref[...]) * _softplus(abt[0:B_HEADS, :] + dtbt_ref[...])

    ri = lax.broadcasted_iota(jnp.int32, (TT, TT), 0)
    ci = lax.broadcasted_iota(jnp.int32, (TT, TT), 1)
    same = (ri // C) == (ci // C)
    causal = jnp.logical_and(same, ci <= ri)
    strict = jnp.logical_and(same, ci < ri)
    low = jnp.where(causal, 1.0, 0.0)
    upp = jnp.where(jnp.logical_and(same, ri <= ci), 1.0, 0.0)
    eye = jnp.where(ci == ri, 1.0, 0.0)
    zpad = jnp.zeros((dk - C, dk), F32)

    def conv_silu(col0):
        acc = jnp.zeros((TT, dk), F32)
        for j in range(CONV_WIDTH):
            r0 = 8 - (CONV_WIDTH - 1) + j
            acc = acc + xpad_s[r0:r0 + TT, col0:col0 + dk] * cw_ref[j:j + 1, col0:col0 + dk]
        return acc * _sigmoid(acc)

    heads = range(B_HEADS)
    q = jnp.stack([conv_silu(h * dk) for h in heads])
    k = jnp.stack([conv_silu(B_WIDTH + h * dk) for h in heads])
    v = jnp.stack([conv_silu(2 * B_WIDTH + h * dk) for h in heads])
    q = q * lax.rsqrt(jnp.sum(q * q, axis=-1, keepdims=True) + RMS_EPS) * (dk ** -0.5)
    k = k * lax.rsqrt(jnp.sum(k * k, axis=-1, keepdims=True) + RMS_EPS)
    la_c = jnp.stack([la_cols[:, h:h + 1] for h in heads])
    be = jnp.stack([be_cols[:, h:h + 1] for h in heads])
    la_r = jnp.stack([la_rows[h:h + 1, :] for h in heads])

    g_col = jnp.sum(low[None] * la_r, axis=2, keepdims=True)
    g_row = jnp.sum(upp[None] * la_c, axis=1, keepdims=True)
    diff = g_col - g_row
    decay = jnp.where(causal[None], jnp.exp(jnp.where(causal[None], diff, 0.0)), 0.0)
    kb = k * be
    mm = jnp.where(strict[None], -_bmm3(_split(kb), _split(k), nt=True) * decay, 0.0)
    tinv = eye[None] + mm
    pw = _split(mm)
    for _ in range(int(math.log2(C)) - 1):
        pw = _split(_bmm3(pw, pw))
        tinv = tinv + _bmm3(_split(tinv), pw)
    eg = jnp.exp(g_col)
    ts = _split(tinv)
    u = _bmm3(ts, _split(v * be))
    w = _bmm3(ts, _split(kb * eg))
    a_intra = _bmm3(_split(q), _split(k), nt=True) * decay
    qe = q * eg

    S = state_s[...]
    v_new, o_state = [], []
    for c in range(TT // C):
        rs = slice(c * C, (c + 1) * C)
        Ss = _split(S)
        vn = u[:, rs] - _bmm3(_split(w[:, rs]), Ss)
        o_state.append(_bmm3(_split(qe[:, rs]), Ss))
        g_last = g_col[:, c * C + C - 1:c * C + C, :]
        kd = k[:, rs] * jnp.exp(g_last - g_col[:, rs])
        kdt = jnp.stack([jnp.concatenate([kd[h], zpad], axis=0).T for h in heads])
        vnp = jnp.stack([jnp.concatenate([vn[h], zpad], axis=0) for h in heads])
        S = S * jnp.exp(g_last) + _bmm3(_split(kdt), _split(vnp))
        v_new.append(vn)
    state_s[...] = S
    o = jnp.concatenate(o_state, axis=1) + _bmm3(_split(a_intra), _split(jnp.concatenate(v_new, axis=1)))

    on = o * lax.rsqrt(jnp.mean(o * o, axis=-1, keepdims=True) + RMS_EPS) * gn_ref[...]
    for h in heads:
        z = z_ref[0, :, h * dk:(h + 1) * dk]
        o_ref[0, :, h * dk:(h + 1) * dk] = (on[h] * (z * _sigmoid(z))).astype(o_ref.dtype)

    xpad_s[0:8, :] = xpad_s[TT:TT + 8, :]


def _gdn(xqkv, z, ab, abt, conv_w, a_log, dt_bias, gdn_norm):
    B, T, _ = xqkv.shape
    TT = 256
    full = lambda shape: pl.BlockSpec(shape, lambda b, t: (0,) * len(shape))
    return pl.pallas_call(
        functools.partial(_gdn_kernel, tt_rows=TT),
        grid=(B, T // TT),
        in_specs=[pl.BlockSpec((1, TT, 3 * B_WIDTH), lambda b, t: (b, t, 0)),
                  pl.BlockSpec((1, TT, B_WIDTH), lambda b, t: (b, t, 0)),
                  pl.BlockSpec((1, TT, 2 * B_HEADS), lambda b, t: (b, t, 0)),
                  pl.BlockSpec((1, 2 * B_HEADS, TT), lambda b, t: (b, 0, t)),
                  full((CONV_WIDTH, 3 * B_WIDTH)),
                  full((1, B_HEADS)), full((1, B_HEADS)), full((B_HEADS, 1)), full((B_HEADS, 1)),
                  full((1, B_HEAD_DIM))],
        out_specs=pl.BlockSpec((1, TT, B_WIDTH), lambda b, t: (b, t, 0)),
        out_shape=jax.ShapeDtypeStruct((B, T, B_WIDTH), BF16),
        scratch_shapes=[pltpu.VMEM((TT + 8, 3 * B_WIDTH), F32),
                        pltpu.VMEM((B_HEADS, B_HEAD_DIM, B_HEAD_DIM), F32)],
        compiler_params=_cparams(("parallel", "arbitrary")),
        name="gdn",
    )(xqkv, z, ab, abt, conv_w, a_log.reshape(1, -1), dt_bias.reshape(1, -1),
      a_log.reshape(-1, 1), dt_bias.reshape(-1, 1), gdn_norm.reshape(1, -1))


def _dsa_prep_kernel(ckv_ref, small_ref, kvn_ref, lng_ref, lnb_ref, cn_ref, cnt_ref, kin_ref):
    c = ckv_ref[0]
    ms = jnp.mean(c * c, axis=-1, keepdims=True)
    cn = c * lax.rsqrt(ms + RMS_EPS) * kvn_ref[...]
    cn_ref[0] = cn.astype(cn_ref.dtype)
    for r in range(cn.shape[0] // DSA_KW):
        cnt_ref[0, r] = cn[r * DSA_KW:(r + 1) * DSA_KW].T.astype(cnt_ref.dtype)
    ki = small_ref[0, :, SMALL_KI:SMALL_KI + IDX_DIM]
    kin_ref[0] = _layer_norm(ki, lng_ref[...], lnb_ref[...]).astype(kin_ref.dtype)


def _dsa_prep(ckv, small, kv_norm, ln_g, ln_b):
    B, T, _ = ckv.shape
    tm = 1024
    vec = lambda n: pl.BlockSpec((1, n), lambda b, i: (0, 0))
    return pl.pallas_call(
        _dsa_prep_kernel,
        grid=(B, T // tm),
        in_specs=[pl.BlockSpec((1, tm, KV_RANK), lambda b, i: (b, i, 0)),
                  pl.BlockSpec((1, tm, SMALL_W), lambda b, i: (b, i, 0)),
                  vec(KV_RANK), vec(IDX_DIM), vec(IDX_DIM)],
        out_specs=[pl.BlockSpec((1, tm, KV_RANK), lambda b, i: (b, i, 0)),
                   pl.BlockSpec((1, tm // DSA_KW, KV_RANK, DSA_KW), lambda b, i: (b, i, 0, 0)),
                   pl.BlockSpec((1, tm, IDX_DIM), lambda b, i: (b, i, 0))],
        out_shape=[jax.ShapeDtypeStruct((B, T, KV_RANK), BF16),
                   jax.ShapeDtypeStruct((B, T // DSA_KW, KV_RANK, DSA_KW), BF16),
                   jax.ShapeDtypeStruct((B, T, IDX_DIM), BF16)],
        compiler_params=_cparams(("parallel", "parallel")),
        name="dsa_prep",
    )(ckv, small, kv_norm.reshape(1, -1), ln_g.reshape(1, -1), ln_b.reshape(1, -1))


def _key_to_f32(k):
    return lax.bitcast_convert_type(jnp.where(k >= 0, k, k ^ 0x7FFFFFFF), F32)


def _dsa_kernel(tab_ref, qc_ref, qi_ref, small_ref, cn_ref, cnt_ref, kin_ref, wuk_ref, wuvt_ref, bias_ref,
                o_ref, sc_s, acc_s, qlt_s, m_s, l_s, *, topk):
    j = pl.program_id(1)
    QB, KW, H, R, dh, DI = DSA_QB, DSA_KW, C_HEADS, KV_RANK, C_HEAD_DIM, IDX_DIM
    W = H * QB
    nck = (j * QB + QB - 1) // KW + 1
    nfar = jnp.maximum((j - 1) // 2, 0)

    qct = qc_ref[0].astype(F32).T.astype(BF16)
    hg = H // DSA_GROUPS
    for h in range(H):
        qlt_s[h // hg, :, (h % hg) * QB:(h % hg + 1) * QB] = (
            _dot(wuk_ref[h], qct[h * dh:(h + 1) * dh]).astype(BF16))

    qit = qi_ref[0].astype(F32).T.astype(BF16)
    qi_all = jnp.concatenate([qit[h * DI:(h + 1) * DI] for h in range(IDX_HEADS)], axis=1)
    wt = small_ref[0].T
    w_all = jnp.concatenate([wt[SMALL_WI + h:SMALL_WI + h + 1] for h in range(IDX_HEADS)],
                            axis=1) * IDX_WEIGHT_SCALE
    tk0 = lax.broadcasted_iota(jnp.int32, (KW, QB), 0)
    tq = j * QB + lax.broadcasted_iota(jnp.int32, (KW, QB), 1)

    def idx_body(c, _):
        rows = pl.ds(pl.multiple_of(c * KW, KW), KW)
        s8 = jnp.maximum(_dot(kin_ref[0, rows, :], qi_all), 0.0) * w_all
        sc = s8[:, 0:QB]
        for h in range(1, IDX_HEADS):
            sc = sc + s8[:, h * QB:(h + 1) * QB]
        sc_s[c] = jnp.where(tk0 + c * KW <= tq, sc, -jnp.inf)
        return 0

    lax.fori_loop(0, nck, idx_body, 0)

    ngrp = (nck + DSA_CNT_UNROLL - 1) // DSA_CNT_UNROLL

    def fill_body(c, _):
        sc_s[c] = jnp.full((KW, QB), -jnp.inf, F32)
        return 0

    lax.fori_loop(nck, ngrp * DSA_CNT_UNROLL, fill_body, 0)

    tcol = j * QB + lax.broadcasted_iota(jnp.int32, (1, QB), 1)
    kk = jnp.minimum(tcol + 1, topk).astype(F32)

    def bis_body(_, carry):
        lo, hi, cge, cgt = carry
        mid = (lo >> 1) + (hi >> 1) + (lo & hi & 1) + ((lo ^ hi) & 1)
        midf = _key_to_f32(mid)

        def cnt_body(gi, acc):
            for u in range(DSA_CNT_UNROLL):
                hit = jnp.where(sc_s[gi * DSA_CNT_UNROLL + u] >= midf, 1.0, 0.0)
                acc = acc + jnp.sum(hit.reshape(KW // 8, 8, QB), axis=0)
            return acc

        acc = lax.fori_loop(0, ngrp, cnt_body, jnp.zeros((8, QB), F32))
        cnt = jnp.sum(acc, axis=0, keepdims=True)
        ok = cnt >= kk
        return (jnp.where(ok, mid, lo), jnp.where(ok, hi, mid - 1),
                jnp.where(ok, cnt, cge), jnp.where(ok, cgt, cnt))

    init = (jnp.full((1, QB), KEY_LO, jnp.int32), jnp.full((1, QB), KEY_HI, jnp.int32),
            (tcol + 1).astype(F32), jnp.zeros((1, QB), F32))
    lo, _, cge, cgt = lax.fori_loop(0, 32, bis_body, init)
    tau = _key_to_f32(lo)

    @pl.when(jnp.max(cge - kk) > 0)
    def _():
        need = kk - cgt
        lr = lax.broadcasted_iota(jnp.int32, (KW, KW), 0)
        lc = lax.broadcasted_iota(jnp.int32, (KW, KW), 1)
        lower = jnp.where(lc <= lr, 1.0, 0.0).astype(BF16)

        def tie_body(c, seen):
            s = sc_s[c]
            eq = s == tau
            pc = _dot(lower, jnp.where(eq, 1.0, 0.0).astype(BF16)) + seen
            sc_s[c] = jnp.where(jnp.logical_and(eq, pc > need), -jnp.inf, s)
            return pc[KW - 1:KW, :]

        lax.fori_loop(0, nck, tie_body, jnp.zeros((1, QB), F32))

    m_s[...] = jnp.full(m_s.shape, M_INIT, F32)
    l_s[...] = jnp.zeros(l_s.shape, F32)
    acc_s[...] = jnp.zeros(acc_s.shape, F32)
    sat = jnp.concatenate([jnp.full((1, QB), tab_ref[A_HEADS + h, N_BUCKETS - 1], F32) for h in range(H)],
                          axis=1)

    GW = W // DSA_GROUPS

    def process(c, near):
        rows = pl.ds(pl.multiple_of(c * KW, KW), KW)
        madd = jnp.where(sc_s[c] >= tau, 0.0, NEG)
        maddg = jnp.concatenate([madd] * (H // DSA_GROUPS), axis=1)
        ck = cn_ref[0, rows, :]
        ckt = cnt_ref[0, c]
        for g in range(DSA_GROUPS):
            gs = slice(g * GW, (g + 1) * GW)
            s = _dot(ck, qlt_s[g]) + maddg
            if near:
                s = s + bias_ref[(j * QB - c * KW) // QB, :, gs]
                const = 0.0
            else:
                const = sat[:, gs]
            m_old = m_s[g]
            m_new = jnp.maximum(m_old, jnp.max(s, axis=0, keepdims=True) + const)
            pr = jnp.exp(s - (m_new - const))
            alpha = jnp.exp(m_old - m_new)
            l_s[g] = alpha * l_s[g] + jnp.sum(pr, axis=0, keepdims=True)
            m_s[g] = m_new
            acc_s[g] = alpha * acc_s[g] + _dot(ckt, pr.astype(BF16))
        return 0

    lax.fori_loop(0, nfar, lambda c, _: process(c, False), 0)
    lax.fori_loop(nfar, nck, lambda c, _: process(c, True), 0)

    hg = H // DSA_GROUPS
    octs = []
    for h in range(H):
        g, hs = h // hg, slice((h % hg) * QB, (h % hg + 1) * QB)
        olt = (acc_s[g, :, hs] * (1.0 / l_s[g, :, hs])).astype(BF16)
        octs.append(_dot(wuvt_ref[h], olt))
    o_ref[0] = jnp.concatenate(octs, axis=0).T.astype(o_ref.dtype)


def _dsa(tab, q_c, qi, small, cn, cnt, kin, wuk, wuvt, bias):
    B, T, _ = q_c.shape
    QB, KW = DSA_QB, DSA_KW
    W = C_HEADS * QB
    topk = min(DSA_TOPK_MAX, T // 4)
    assert (T // KW) % DSA_CNT_UNROLL == 0
    row = lambda w: pl.BlockSpec((1, QB, w), lambda b, j: (b, j, 0))
    return pl.pallas_call(
        functools.partial(_dsa_kernel, topk=topk),
        grid=(B, T // QB),
        in_specs=[pl.BlockSpec(memory_space=pltpu.SMEM),
                  row(C_WIDTH), row(IDX_HEADS * IDX_DIM), row(SMALL_W),
                  pl.BlockSpec((1, T, KV_RANK), lambda b, j: (b, 0, 0)),
                  pl.BlockSpec((1, T // KW, KV_RANK, KW), lambda b, j: (b, 0, 0, 0)),
                  pl.BlockSpec((1, T, IDX_DIM), lambda b, j: (b, 0, 0)),
                  pl.BlockSpec(wuk.shape, lambda b, j: (0, 0, 0)),
                  pl.BlockSpec(wuvt.shape, lambda b, j: (0, 0, 0)),
                  pl.BlockSpec(bias.shape, lambda b, j: (0, 0, 0))],
        out_specs=row(C_WIDTH),
        out_shape=jax.ShapeDtypeStruct((B, T, C_WIDTH), BF16),
        scratch_shapes=[pltpu.VMEM((T // KW, KW, QB), F32),
                        pltpu.VMEM((DSA_GROUPS, KV_RANK, W // DSA_GROUPS), F32),
                        pltpu.VMEM((DSA_GROUPS, KV_RANK, W // DSA_GROUPS), BF16),
                        pltpu.VMEM((DSA_GROUPS, 1, W // DSA_GROUPS), F32),
                        pltpu.VMEM((DSA_GROUPS, 1, W // DSA_GROUPS), F32)],
        compiler_params=_cparams(("parallel", "arbitrary")),
        name="dsa",
    )(tab, q_c, qi, small, cn, cnt, kin, wuk, wuvt, bias)


def _merge_kernel(x_ref, oa_ref, ob_ref, oc_ref, g_ref, wa_ref, wb_ref, wc_ref, wo_ref, lg_ref, lb_ref,
                  out_ref):
    D = D_MODEL
    y = _sigmoid(g_ref[:, 0:D].astype(F32)) * _dot(oa_ref[...], wa_ref[...])
    y = y + _sigmoid(g_ref[:, D:2 * D].astype(F32)) * _dot(ob_ref[...], wb_ref[...])
    y = y + _sigmoid(g_ref[:, 2 * D:3 * D].astype(F32)) * _dot(oc_ref[...], wc_ref[...])
    mix = _dot(y.astype(BF16), wo_ref[...])
    out_ref[...] = _layer_norm(DEEPNORM_ALPHA * x_ref[...] + mix, lg_ref[...], lb_ref[...])


def _merge(x2, oa, ob, oc, gates, wa, wb, wc, wo, ln_g, ln_b):
    n = x2.shape[0]
    tm = 512
    row = lambda w: pl.BlockSpec((tm, w), lambda i: (i, 0))
    full = lambda a: pl.BlockSpec(a.shape, lambda i: (0, 0))
    lg, lb = ln_g.reshape(1, -1), ln_b.reshape(1, -1)
    return pl.pallas_call(
        _merge_kernel,
        grid=(n // tm,),
        in_specs=[row(D_MODEL), row(A_WIDTH), row(B_WIDTH), row(C_WIDTH), row(3 * D_MODEL),
                  full(wa), full(wb), full(wc), full(wo), full(lg), full(lb)],
        out_specs=row(D_MODEL),
        out_shape=jax.ShapeDtypeStruct((n, D_MODEL), F32),
        compiler_params=_cparams(("parallel",)),
        name="merge",
    )(x2, oa, ob, oc, gates, wa, wb, wc, wo, lg, lb)


def _mlp_kernel(x_ref, wu_ref, wd_ref, lg_ref, lb_ref, out_ref):
    x = x_ref[...]
    xb = x.astype(BF16)
    acc = jnp.zeros(x.shape, F32)
    fc = 1024
    for c in range(D_FF // fc):
        h = jnp.maximum(_dot(xb, wu_ref[:, c * fc:(c + 1) * fc]), 0.0)
        acc = acc + _dot((h * h).astype(BF16), wd_ref[c * fc:(c + 1) * fc, :])
    out_ref[...] = _layer_norm(DEEPNORM_ALPHA * x + acc, lg_ref[...], lb_ref[...])


def _mlp(x2, wu, wd, ln_g, ln_b):
    n = x2.shape[0]
    tm = 512
    lg, lb = ln_g.reshape(1, -1), ln_b.reshape(1, -1)
    return pl.pallas_call(
        _mlp_kernel,
        grid=(n // tm,),
        in_specs=[pl.BlockSpec((tm, D_MODEL), lambda i: (i, 0)),
                  pl.BlockSpec(wu.shape, lambda i: (0, 0), pipeline_mode=pl.Buffered(1)),
                  pl.BlockSpec(wd.shape, lambda i: (0, 0), pipeline_mode=pl.Buffered(1)),
                  pl.BlockSpec(lg.shape, lambda i: (0, 0)),
                  pl.BlockSpec(lb.shape, lambda i: (0, 0))],
        out_specs=pl.BlockSpec((tm, D_MODEL), lambda i: (i, 0)),
        out_shape=jax.ShapeDtypeStruct((n, D_MODEL), F32),
        compiler_params=_cparams(("parallel",)),
        name="mlp",
    )(x2, wu, wd, lg, lb)


def _prep_w_uk(w_uk):
    return (w_uk * (w_uk.shape[-1] ** -0.5)).transpose(1, 0, 2).astype(BF16)


def _prep_w_uv(w_uv):
    return w_uv.transpose(1, 2, 0).astype(BF16)


def _mixer_branches(x2, B, T, w_pad, conv_w, a_log, dt_bias, gdn_norm, kv_norm, ln_g, ln_b,
                    wuk, wuvt, tab, bias_a, bias_c):
    qkv_a, qkv_b, z_b, q_c, ckv, qi, small, gates = _in_proj(x2, w_pad)

    o_a = _moba(qkv_a.reshape(B, T, -1), tab, bias_a).reshape(B * T, A_WIDTH)

    ab = small[:, SMALL_A:SMALL_A + 2 * B_HEADS].reshape(B, T, 2 * B_HEADS)
    o_b = _gdn(qkv_b.reshape(B, T, -1), z_b.reshape(B, T, -1), ab, ab.transpose(0, 2, 1),
               conv_w, a_log, dt_bias, gdn_norm).reshape(B * T, B_WIDTH)

    small3 = small.reshape(B, T, SMALL_W)
    cn, cnt, kin = _dsa_prep(ckv.reshape(B, T, KV_RANK), small3, kv_norm, ln_g, ln_b)
    o_c = _dsa(tab, q_c.reshape(B, T, -1), qi.reshape(B, T, -1), small3, cn, cnt, kin,
               wuk, wuvt, bias_c).reshape(B * T, C_WIDTH)
    return o_a, o_b, o_c, gates


def kernel(x, rel_bias, w_in, conv_w, a_log, dt_bias, gdn_norm, kv_norm, idx_k_ln_g, idx_k_ln_b, w_uk, w_uv, w_branch_a, w_branch_b, w_branch_c, w_out, ln1_g, ln1_b, w_up, w_down, ln2_g, ln2_b):
    B, T, D = x.shape
    bias_a, bias_c = _bias_tiles(rel_bias)
    tab = rel_bias.T
    x2 = x.reshape(B * T, D)
    for i in range(w_in.shape[0]):
        o_a, o_b, o_c, gates = _mixer_branches(
            x2, B, T, _prep_w_in(w_in[i]), conv_w[i], a_log[i], dt_bias[i], gdn_norm[i], kv_norm[i],
            idx_k_ln_g[i], idx_k_ln_b[i], _prep_w_uk(w_uk[i]), _prep_w_uv(w_uv[i]),
            tab, bias_a, bias_c)
        x2 = _merge(x2, o_a, o_b, o_c, gates, w_branch_a[i].astype(BF16), w_branch_b[i].astype(BF16),
                    w_branch_c[i].astype(BF16), w_out[i].astype(BF16), ln1_g[i], ln1_b[i])
        x2 = _mlp(x2, w_up[i].astype(BF16), w_down[i].astype(BF16), ln2_g[i], ln2_b[i])
    return x2.reshape(B, T, D)
```

```python
import functools
import math

import numpy as np
import jax
import jax.numpy as jnp
from jax import lax
from jax.experimental import pallas as pl
from jax.experimental.pallas import tpu as pltpu

D_MODEL = 1024
DEPTH = 4
A_HEADS = 8
A_HEAD_DIM = 64
A_WIDTH = A_HEADS * A_HEAD_DIM
MOBA_BLOCK = 256
MOBA_TOPK = 3
B_HEADS = 4
B_HEAD_DIM = 128
B_WIDTH = B_HEADS * B_HEAD_DIM
CONV_WIDTH = 4
GDN_CHUNK = 64
C_HEADS = 8
C_HEAD_DIM = 64
C_WIDTH = C_HEADS * C_HEAD_DIM
KV_RANK = 256
IDX_HEADS = 8
IDX_DIM = 64
DSA_TOPK_MAX = 256
IDX_WEIGHT_SCALE = IDX_HEADS ** -0.5 * IDX_DIM ** -0.5
N_BUCKETS = 32
MAX_DISTANCE = 128
D_FF = 4 * D_MODEL
DEEPNORM_ALPHA = (2 * DEPTH) ** 0.25
LN_EPS = 1e-5
RMS_EPS = 1e-6

IN_SPLIT = (A_WIDTH, A_WIDTH, A_WIDTH,
            B_WIDTH, B_WIDTH, B_WIDTH, B_WIDTH, B_HEADS, B_HEADS,
            C_WIDTH, KV_RANK, IDX_HEADS * IDX_DIM, IDX_DIM, IDX_HEADS,
            D_MODEL, D_MODEL, D_MODEL)

F32 = jnp.float32
BF16 = jnp.bfloat16
NEG = -1e9
LOG2E = math.log2(math.e)
MOBA_ACC_PAD = 16
DSA_ACC_PAD = 16
M_INIT = -1e8
LANES = 128
DSA_QB = 128
DSA_KW = 256
DSA_CNT_UNROLL = 4
SMALL_W = 128
SMALL_KI = 0
SMALL_WI = IDX_DIM
SMALL_A = SMALL_WI + IDX_HEADS
SMALL_B = SMALL_A + B_HEADS
VMEM_LIMIT = 56 * 1024 * 1024

KEY_LO = int(np.int32(np.uint32(0xFF7FFFFF) ^ np.uint32(0x7FFFFFFF)))
KEY_HI = 0x7F800000


def _cparams(sem):
    return pltpu.CompilerParams(dimension_semantics=sem, vmem_limit_bytes=VMEM_LIMIT)


def _dot(a, b):
    return jnp.dot(a, b, preferred_element_type=F32)


def _dot_nt(a, b):
    return lax.dot_general(a, b, (((1,), (1,)), ((), ())), preferred_element_type=F32)


def _b16(a):
    return a.astype(BF16)


def _bmm(a, b, nt=False):
    dims = (((2,), (2 if nt else 1,)), ((0,), (0,)))
    return lax.dot_general(a, b, dims, preferred_element_type=F32)


def _layer_norm(r, g, b):
    mu = jnp.mean(r, axis=-1, keepdims=True)
    d = r - mu
    var = jnp.mean(d * d, axis=-1, keepdims=True)
    return d * lax.rsqrt(var + LN_EPS) * g + b


def _sigmoid(x):
    return 1.0 / (1.0 + jnp.exp(-x))


def _t5_bucket_np(d):
    max_exact = N_BUCKETS // 2
    d = np.maximum(d, 0)
    large = max_exact + (np.log(np.maximum(d, max_exact).astype(np.float32) / max_exact)
                         / math.log(MAX_DISTANCE / max_exact) * (N_BUCKETS - max_exact)).astype(np.int32)
    return np.where(d < max_exact, d, np.minimum(large, N_BUCKETS - 1))


def _bucket_tiles(qb, w, offs):
    tq = np.arange(qb)[:, None]
    tk = np.arange(w)[None, :]
    tiles = []
    for off in offs:
        d = off + tq - tk
        tiles.append(np.where(d >= 0, _t5_bucket_np(d), -1))
    return np.stack(tiles).astype(np.int32)


def _bias_kernel(tab_ref, mi_ref, di_ref, mo_ref, do_ref):
    h = pl.program_id(0)

    def build(idx, col):
        acc = jnp.full(idx.shape, NEG, F32)
        for b in range(N_BUCKETS):
            acc = jnp.where(idx == b, tab_ref[col, b], acc)
        return acc

    mo_ref[...] = build(mi_ref[...], h) * LOG2E
    do_ref[...] = build(di_ref[...], A_HEADS + h) * LOG2E


def _bias_tiles(rel_bias):
    mi = jnp.asarray(_bucket_tiles(MOBA_BLOCK, MOBA_BLOCK, (0, MOBA_BLOCK)).transpose(0, 2, 1))
    di = jnp.asarray(_bucket_tiles(DSA_QB, DSA_KW, tuple(t * DSA_QB for t in range(4))).transpose(0, 2, 1))
    return pl.pallas_call(
        _bias_kernel,
        grid=(A_HEADS,),
        in_specs=[pl.BlockSpec(memory_space=pltpu.SMEM),
                  pl.BlockSpec(mi.shape, lambda h: (0, 0, 0)),
                  pl.BlockSpec(di.shape, lambda h: (0, 0, 0))],
        out_specs=[pl.BlockSpec(mi.shape, lambda h: (0, 0, h)),
                   pl.BlockSpec(di.shape, lambda h: (0, 0, h))],
        out_shape=[jax.ShapeDtypeStruct(mi.shape[:2] + (A_HEADS * mi.shape[2],), F32),
                   jax.ShapeDtypeStruct(di.shape[:2] + (C_HEADS * di.shape[2],), F32)],
        compiler_params=_cparams(("arbitrary",)),
        name="bias_tiles",
    )(rel_bias.T, mi, di)


IN_SEGS = (("qkv_a", 3 * A_WIDTH, BF16), ("qkv_b", 3 * B_WIDTH, F32), ("z_b", B_WIDTH, F32),
           ("q_c", C_WIDTH, BF16), ("ckv", KV_RANK, F32), ("qi", IDX_HEADS * IDX_DIM, BF16),
           ("small", SMALL_W, F32), ("gates", 3 * D_MODEL, BF16))
IN_PAD_WIDTH = sum(s[1] for s in IN_SEGS)


def _prep_w_in(w):
    cs = np.cumsum((0,) + IN_SPLIT)

    def seg(i):
        return w[:, cs[i]:cs[i + 1]]

    pad = jnp.zeros((w.shape[0], SMALL_W - (IDX_DIM + IDX_HEADS + 2 * B_HEADS)), w.dtype)
    small = jnp.concatenate([seg(12), seg(13), seg(7), seg(8), pad], axis=1)
    return jnp.concatenate([w[:, cs[0]:cs[3]], w[:, cs[3]:cs[6]], seg(6), seg(9), seg(10), seg(11),
                            small, w[:, cs[14]:cs[17]]], axis=1).astype(BF16)


def _in_proj_kernel(x_ref, w_ref, *out_refs):
    xb = x_ref[...].astype(BF16)
    off = 0
    for (_, width, _), o_ref in zip(IN_SEGS, out_refs):
        for c0 in range(0, width, 512):
            cw = min(512, width - c0)
            o_ref[:, c0:c0 + cw] = _dot(xb, w_ref[:, off + c0:off + c0 + cw]).astype(o_ref.dtype)
        off += width


def _in_proj(x2, w_pad):
    n = x2.shape[0]
    tm = 512
    return pl.pallas_call(
        _in_proj_kernel,
        grid=(n // tm,),
        in_specs=[pl.BlockSpec((tm, D_MODEL), lambda i: (i, 0)),
                  pl.BlockSpec((D_MODEL, IN_PAD_WIDTH), lambda i: (0, 0), pipeline_mode=pl.Buffered(1))],
        out_specs=[pl.BlockSpec((tm, wd), lambda i: (i, 0)) for _, wd, _ in IN_SEGS],
        out_shape=[jax.ShapeDtypeStruct((n, wd), dt) for _, wd, dt in IN_SEGS],
        compiler_params=_cparams(("parallel",)),
        name="in_proj",
    )(x2, w_pad)


def _moba_kernel(tab_ref, q_ref, k_ref, v_ref, bias_ref, o_ref,
                 kmean_s, vt_s, qbd_s, cm_s, m_s, acc_s, sa_s, sb_s, *, nb):
    i = pl.program_id(1)
    BS, H, dh = MOBA_BLOCK, A_HEADS, A_HEAD_DIM
    dhp = dh + MOBA_ACC_PAD

    @pl.when(i == 0)
    def _():
        ones_row = jnp.where(lax.broadcasted_iota(jnp.int32, (MOBA_ACC_PAD, BS), 0) == 0, 1.0, 0.0)

        def prep(n, _):
            rows = pl.ds(pl.multiple_of(n * BS, BS), BS)
            kmean_s[pl.ds(n, 1), :] = jnp.mean(k_ref[0, rows, :].astype(F32), axis=0, keepdims=True)
            vt = v_ref[0, rows, :].astype(F32).T
            parts = []
            for h in range(H):
                parts += [vt[h * dh:(h + 1) * dh], ones_row]
            vt_s[n] = jnp.concatenate(parts, axis=0).astype(BF16)
            return 0

        lax.fori_loop(0, nb, prep, 0)

    qt = (q_ref[0].astype(F32) * (dh ** -0.5 * LOG2E)).T.astype(BF16)
    zeros = jnp.zeros((dh, BS), BF16)
    for p in range(H // 2):
        top = jnp.concatenate([qt[2 * p * dh:(2 * p + 1) * dh], zeros], axis=1)
        bot = jnp.concatenate([zeros, qt[(2 * p + 1) * dh:(2 * p + 2) * dh]], axis=1)
        qbd_s[p] = jnp.concatenate([top, bot], axis=0)

    def all_heads(lhs_fn):
        return jnp.concatenate([lhs_fn(p) for p in range(H // 2)], axis=1)

    def gate_pair(p):
        km = kmean_s[:, p * LANES:(p + 1) * LANES]
        km_hi = km.astype(BF16)
        km_lo = (km - km_hi.astype(F32)).astype(BF16)
        return _dot(km_hi, qbd_s[p]) + _dot(km_lo, qbd_s[p])

    W = H * BS
    blk = lax.broadcasted_iota(jnp.int32, (nb, W), 0)
    valid = blk < i
    g = jnp.where(valid, all_heads(gate_pair), -jnp.inf)
    rank = jnp.zeros((nb, W), jnp.int32)
    for m in range(nb):
        gm = g[m:m + 1, :]
        beats = jnp.logical_or(gm > g, jnp.logical_and(gm == g, m < blk))
        rank = rank + jnp.where(beats, 1, 0)
    sel = jnp.logical_and(valid, rank < MOBA_TOPK)
    sat = jnp.concatenate([jnp.full((1, BS), tab_ref[h, N_BUCKETS - 1] * LOG2E, F32) for h in range(H)],
                          axis=1)
    far_bias = jnp.where(blk < i - 1, sat, 0.0)
    cm_s[0:nb, :] = jnp.where(sel, far_bias, jnp.where(blk == i, 0.0, NEG))
    cm_s[nb:nb + 1, :] = jnp.full((1, W), NEG, F32)

    m_s[...] = jnp.full(m_s.shape, M_INIT, F32)
    acc_s[...] = jnp.zeros(acc_s.shape, F32)

    def produce(n, s_ref):
        rows = pl.ds(pl.multiple_of(jnp.minimum(n, nb - 1) * BS, BS), BS)
        s_ref[...] = all_heads(lambda p: _dot(k_ref[0, rows, p * LANES:(p + 1) * LANES], qbd_s[p]))

    def consume(n, crow, s_ref, tile):
        s = s_ref[...]
        if tile is not None:
            s = s + bias_ref[tile]
        c = cm_s[pl.ds(crow, 1), :]
        m_old = m_s[...]
        m_new = jnp.maximum(m_old, jnp.max(s, axis=0, keepdims=True) + c)
        pb = jnp.exp2(s - (m_new - c)).astype(BF16)
        alpha = jnp.exp2(m_old - m_new)
        m_s[...] = m_new
        nn = jnp.minimum(n, nb - 1)
        for h in range(H):
            hs = slice(h * dhp, (h + 1) * dhp)
            qs = slice(h * BS, (h + 1) * BS)
            acc_s[hs, :] = alpha[:, qs] * acc_s[hs, :] + _dot(vt_s[nn, hs, :], pb[:, qs])

    produce(i, sa_s)
    consume(i, i, sa_s, 0)

    @pl.when(i >= 1)
    def _():
        produce(i - 1, sb_s)
        consume(i - 1, i - 1, sb_s, 1)

    nfar = jnp.maximum(i - 1, 0)
    crow = lambda n: jnp.where(n < nfar, n, nb)
    produce(0, sa_s)

    def pair_body(p, _):
        produce(2 * p + 1, sb_s)
        consume(2 * p, crow(2 * p), sa_s, None)
        produce(2 * p + 2, sa_s)
        consume(2 * p + 1, crow(2 * p + 1), sb_s, None)
        return 0

    lax.fori_loop(0, (nfar + 1) // 2, pair_body, 0)

    parts = []
    for h in range(H):
        inv = 1.0 / acc_s[h * dhp + dh:h * dhp + dh + 1, :]
        parts.append(acc_s[h * dhp:h * dhp + dh, :] * inv)
    o_ref[0] = jnp.concatenate(parts, axis=0).T.astype(o_ref.dtype)


def _moba(qkv, tab, bias):
    B, T, _ = qkv.shape
    BS = MOBA_BLOCK
    nb = T // BS
    W = A_HEADS * BS
    vt_rows = A_HEADS * (A_HEAD_DIM + MOBA_ACC_PAD)
    return pl.pallas_call(
        functools.partial(_moba_kernel, nb=nb),
        grid=(B, nb),
        in_specs=[pl.BlockSpec(memory_space=pltpu.SMEM),
                  pl.BlockSpec((1, BS, A_WIDTH), lambda b, i: (b, i, 0)),
                  pl.BlockSpec((1, T, A_WIDTH), lambda b, i: (b, 0, 1)),
                  pl.BlockSpec((1, T, A_WIDTH), lambda b, i: (b, 0, 2)),
                  pl.BlockSpec(bias.shape, lambda b, i: (0, 0, 0))],
        out_specs=pl.BlockSpec((1, BS, A_WIDTH), lambda b, i: (b, i, 0)),
        out_shape=jax.ShapeDtypeStruct((B, T, A_WIDTH), BF16),
        scratch_shapes=[pltpu.VMEM((nb, A_WIDTH), F32),
                        pltpu.VMEM((nb, vt_rows, BS), BF16),
                        pltpu.VMEM((A_HEADS // 2, 2 * A_HEAD_DIM, 2 * BS), BF16),
                        pltpu.VMEM((nb + 8, W), F32),
                        pltpu.VMEM((1, W), F32),
                        pltpu.VMEM((vt_rows, BS), F32),
                        pltpu.VMEM((BS, W), F32),
                        pltpu.VMEM((BS, W), F32)],
        compiler_params=_cparams(("parallel", "arbitrary")),
        name="moba",
    )(tab, qkv, qkv, qkv, bias)


def _softplus(x):
    return jnp.maximum(x, 0.0) + jnp.log(1.0 + jnp.exp(-jnp.abs(x)))


def _gdn_kernel(x_ref, z_ref, ab_ref, abt_ref, cw_ref, alog_ref, dtb_ref, alogt_ref, dtbt_ref, gn_ref,
                o_ref, xpad_s, state_s, *, tt_rows):
    tt = pl.program_id(1)
    C = GDN_CHUNK
    dk = B_HEAD_DIM
    TT = tt_rows

    @pl.when(tt == 0)
    def _():
        state_s[...] = jnp.zeros_like(state_s)
        xpad_s[0:8, :] = jnp.zeros((8, 3 * B_WIDTH), F32)

    xpad_s[8:8 + TT, :] = x_ref[0]

    ab = ab_ref[0]
    abt = abt_ref[0]
    la_cols = -jnp.exp(alog_ref[...]) * _softplus(ab[:, 0:B_HEADS] + dtb_ref[...])
    be_cols = _sigmoid(ab[:, B_HEADS:2 * B_HEADS])
    la_rows = -jnp.exp(alogt_ref[...]) * _softplus(abt[0:B_HEADS, :] + dtbt_ref[...])

    ri = lax.broadcasted_iota(jnp.int32, (TT, TT), 0)
    ci = lax.broadcasted_iota(jnp.int32, (TT, TT), 1)
    same = (ri // C) == (ci // C)
    causal = jnp.logical_and(same, ci <= ri)
    strict = jnp.logical_and(same, ci < ri)
    low = jnp.where(causal, 1.0, 0.0)
    upp = jnp.where(jnp.logical_and(same, ri <= ci), 1.0, 0.0)
    eye = jnp.where(ci == ri, 1.0, 0.0)
    zpad = jnp.zeros((dk - C, dk), F32)

    def conv_silu(col0):
        acc = jnp.zeros((TT, dk), F32)
        for j in range(CONV_WIDTH):
            r0 = 8 - (CONV_WIDTH - 1) + j
            acc = acc + xpad_s[r0:r0 + TT, col0:col0 + dk] * cw_ref[j:j + 1, col0:col0 + dk]
        return acc * _sigmoid(acc)

    heads = range(B_HEADS)
    q = jnp.stack([conv_silu(h * dk) for h in heads])
    k = jnp.stack([conv_silu(B_WIDTH + h * dk) for h in heads])
    v = jnp.stack([conv_silu(2 * B_WIDTH + h * dk) for h in heads])
    q = q * lax.rsqrt(jnp.sum(q * q, axis=-1, keepdims=True) + RMS_EPS) * (dk ** -0.5)
    k = k * lax.rsqrt(jnp.sum(k * k, axis=-1, keepdims=True) + RMS_EPS)
    la_c = jnp.stack([la_cols[:, h:h + 1] for h in heads])
    be = jnp.stack([be_cols[:, h:h + 1] for h in heads])
    la_r = jnp.stack([la_rows[h:h + 1, :] for h in heads])

    g_col = jnp.sum(low[None] * la_r, axis=2, keepdims=True)
    g_row = jnp.sum(upp[None] * la_c, axis=1, keepdims=True)
    diff = g_col - g_row
    decay = jnp.where(causal[None], jnp.exp(jnp.where(causal[None], diff, 0.0)), 0.0)
    kb = k * be
    mm = jnp.where(strict[None], -_bmm(_b16(kb), _b16(k), nt=True) * decay, 0.0)
    tinv = eye[None] + mm
    pw = _b16(mm)
    for _ in range(int(math.log2(C)) - 1):
        pw = _b16(_bmm(pw, pw))
        tinv = tinv + _bmm(_b16(tinv), pw)
    eg = jnp.exp(g_col)
    ts = _b16(tinv)
    u = _bmm(ts, _b16(v * be))
    w = _bmm(ts, _b16(kb * eg))
    a_intra = _bmm(_b16(q), _b16(k), nt=True) * decay
    qe = q * eg

    S = state_s[...]
    v_new, o_state = [], []
    for c in range(TT // C):
        rs = slice(c * C, (c + 1) * C)
        Ss = _b16(S)
        vn = u[:, rs] - _bmm(_b16(w[:, rs]), Ss)
        o_state.append(_bmm(_b16(qe[:, rs]), Ss))
        g_last = g_col[:, c * C + C - 1:c * C + C, :]
        kd = k[:, rs] * jnp.exp(g_last - g_col[:, rs])
        kdt = jnp.stack([jnp.concatenate([kd[h], zpad], axis=0).T for h in heads])
        vnp = jnp.stack([jnp.concatenate([vn[h], zpad], axis=0) for h in heads])
        S = S * jnp.exp(g_last) + _bmm(_b16(kdt), _b16(vnp))
        v_new.append(vn)
    state_s[...] = S
    o = jnp.concatenate(o_state, axis=1) + _bmm(_b16(a_intra), _b16(jnp.concatenate(v_new, axis=1)))

    on = o * lax.rsqrt(jnp.mean(o * o, axis=-1, keepdims=True) + RMS_EPS) * gn_ref[...]
    for h in heads:
        z = z_ref[0, :, h * dk:(h + 1) * dk]
        o_ref[0, :, h * dk:(h + 1) * dk] = (on[h] * (z * _sigmoid(z))).astype(o_ref.dtype)

    xpad_s[0:8, :] = xpad_s[TT:TT + 8, :]


def _gdn(xqkv, z, ab, abt, conv_w, a_log, dt_bias, gdn_norm):
    B, T, _ = xqkv.shape
    TT = 256
    full = lambda shape: pl.BlockSpec(shape, lambda b, t: (0,) * len(shape))
    return pl.pallas_call(
        functools.partial(_gdn_kernel, tt_rows=TT),
        grid=(B, T // TT),
        in_specs=[pl.BlockSpec((1, TT, 3 * B_WIDTH), lambda b, t: (b, t, 0)),
                  pl.BlockSpec((1, TT, B_WIDTH), lambda b, t: (b, t, 0)),
                  pl.BlockSpec((1, TT, 2 * B_HEADS), lambda b, t: (b, t, 0)),
                  pl.BlockSpec((1, 2 * B_HEADS, TT), lambda b, t: (b, 0, t)),
                  full((CONV_WIDTH, 3 * B_WIDTH)),
                  full((1, B_HEADS)), full((1, B_HEADS)), full((B_HEADS, 1)), full((B_HEADS, 1)),
                  full((1, B_HEAD_DIM))],
        out_specs=pl.BlockSpec((1, TT, B_WIDTH), lambda b, t: (b, t, 0)),
        out_shape=jax.ShapeDtypeStruct((B, T, B_WIDTH), BF16),
        scratch_shapes=[pltpu.VMEM((TT + 8, 3 * B_WIDTH), F32),
                        pltpu.VMEM((B_HEADS, B_HEAD_DIM, B_HEAD_DIM), F32)],
        compiler_params=_cparams(("parallel", "arbitrary")),
        name="gdn",
    )(xqkv, z, ab, abt, conv_w, a_log.reshape(1, -1), dt_bias.reshape(1, -1),
      a_log.reshape(-1, 1), dt_bias.reshape(-1, 1), gdn_norm.reshape(1, -1))


def _dsa_prep_kernel(ckv_ref, small_ref, kvn_ref, lng_ref, lnb_ref, cn_ref, cnt_ref, kin_ref):
    c = ckv_ref[0]
    ms = jnp.mean(c * c, axis=-1, keepdims=True)
    cn = c * lax.rsqrt(ms + RMS_EPS) * kvn_ref[...]
    cn_ref[0] = cn.astype(cn_ref.dtype)
    ones_row = jnp.where(lax.broadcasted_iota(jnp.int32, (DSA_ACC_PAD, DSA_KW), 0) == 0, 1.0, 0.0)
    for r in range(cn.shape[0] // DSA_KW):
        ct = jnp.concatenate([cn[r * DSA_KW:(r + 1) * DSA_KW].T, ones_row], axis=0)
        cnt_ref[0, r] = ct.astype(cnt_ref.dtype)
    ki = small_ref[0, :, SMALL_KI:SMALL_KI + IDX_DIM]
    kin_ref[0] = _layer_norm(ki, lng_ref[...], lnb_ref[...]).astype(kin_ref.dtype)


def _dsa_prep(ckv, small, kv_norm, ln_g, ln_b):
    B, T, _ = ckv.shape
    tm = 1024
    vec = lambda n: pl.BlockSpec((1, n), lambda b, i: (0, 0))
    return pl.pallas_call(
        _dsa_prep_kernel,
        grid=(B, T // tm),
        in_specs=[pl.BlockSpec((1, tm, KV_RANK), lambda b, i: (b, i, 0)),
                  pl.BlockSpec((1, tm, SMALL_W), lambda b, i: (b, i, 0)),
                  vec(KV_RANK), vec(IDX_DIM), vec(IDX_DIM)],
        out_specs=[pl.BlockSpec((1, tm, KV_RANK), lambda b, i: (b, i, 0)),
                   pl.BlockSpec((1, tm // DSA_KW, KV_RANK + DSA_ACC_PAD, DSA_KW), lambda b, i: (b, i, 0, 0)),
                   pl.BlockSpec((1, tm, IDX_DIM), lambda b, i: (b, i, 0))],
        out_shape=[jax.ShapeDtypeStruct((B, T, KV_RANK), BF16),
                   jax.ShapeDtypeStruct((B, T // DSA_KW, KV_RANK + DSA_ACC_PAD, DSA_KW), BF16),
                   jax.ShapeDtypeStruct((B, T, IDX_DIM), BF16)],
        compiler_params=_cparams(("parallel", "parallel")),
        name="dsa_prep",
    )(ckv, small, kv_norm.reshape(1, -1), ln_g.reshape(1, -1), ln_b.reshape(1, -1))


def _key_to_f32(k):
    return lax.bitcast_convert_type(jnp.where(k >= 0, k, k ^ 0x7FFFFFFF), F32)


def _dsa_kernel(qc_ref, qi_ref, small_ref, cn_ref, cnt_ref, kin_ref, wuk_ref, wuvt_ref, bias_ref,
                o_ref, sc_s, acc_s, qlt_s, m_s, sa_s, sb_s, *, topk):
    j = pl.program_id(1)
    QB, KW, H, R, dh, DI = DSA_QB, DSA_KW, C_HEADS, KV_RANK, C_HEAD_DIM, IDX_DIM
    nck = (j * QB + QB - 1) // KW + 1

    qct = qc_ref[0].astype(F32).T.astype(BF16)
    for h in range(H):
        qlt_s[:, h * QB:(h + 1) * QB] = _dot(wuk_ref[h], qct[h * dh:(h + 1) * dh]).astype(BF16)

    qit = qi_ref[0].astype(F32).T.astype(BF16)
    qi_all = jnp.concatenate([qit[h * DI:(h + 1) * DI] for h in range(IDX_HEADS)], axis=1)
    wt = small_ref[0].T
    w_all = jnp.concatenate([wt[SMALL_WI + h:SMALL_WI + h + 1] for h in range(IDX_HEADS)],
                            axis=1) * IDX_WEIGHT_SCALE
    tk0 = lax.broadcasted_iota(jnp.int32, (KW, QB), 0)
    tq = j * QB + lax.broadcasted_iota(jnp.int32, (KW, QB), 1)

    def idx_body(c, _):
        rows = pl.ds(pl.multiple_of(c * KW, KW), KW)
        s8 = jnp.maximum(_dot(kin_ref[0, rows, :], qi_all), 0.0) * w_all
        sc = s8[:, 0:QB]
        for h in range(1, IDX_HEADS):
            sc = sc + s8[:, h * QB:(h + 1) * QB]
        sc_s[c] = jnp.where(tk0 + c * KW <= tq, sc, -jnp.inf)
        return 0

    lax.fori_loop(0, nck, idx_body, 0)

    ngrp = (nck + DSA_CNT_UNROLL - 1) // DSA_CNT_UNROLL

    def fill_body(c, _):
        sc_s[c] = jnp.full((KW, QB), -jnp.inf, F32)
        return 0

    lax.fori_loop(nck, ngrp * DSA_CNT_UNROLL, fill_body, 0)

    tcol = j * QB + lax.broadcasted_iota(jnp.int32, (1, QB), 1)
    kk = jnp.minimum(tcol + 1, topk).astype(F32)

    def bis_body(_, carry):
        lo, hi, cge, cgt = carry
        mid = (lo >> 1) + (hi >> 1) + (lo & hi & 1) + ((lo ^ hi) & 1)
        midf = _key_to_f32(mid)

        def cnt_body(gi, acc):
            for u in range(DSA_CNT_UNROLL):
                hit = jnp.where(sc_s[gi * DSA_CNT_UNROLL + u] >= midf, 1.0, 0.0)
                acc = acc + jnp.sum(hit.reshape(KW // 8, 8, QB), axis=0)
            return acc

        acc = lax.fori_loop(0, ngrp, cnt_body, jnp.zeros((8, QB), F32))
        cnt = jnp.sum(acc, axis=0, keepdims=True)
        ok = cnt >= kk
        return (jnp.where(ok, mid, lo), jnp.where(ok, hi, mid - 1),
                jnp.where(ok, cnt, cge), jnp.where(ok, cgt, cnt))

    init = (jnp.full((1, QB), KEY_LO, jnp.int32), jnp.full((1, QB), KEY_HI, jnp.int32),
            (tcol + 1).astype(F32), jnp.zeros((1, QB), F32))
    lo, _, cge, cgt = lax.fori_loop(0, 32, bis_body, init)
    tau = _key_to_f32(lo)

    @pl.when(jnp.max(cge - kk) > 0)
    def _():
        need = kk - cgt
        lr = lax.broadcasted_iota(jnp.int32, (KW, KW), 0)
        lc = lax.broadcasted_iota(jnp.int32, (KW, KW), 1)
        lower = jnp.where(lc <= lr, 1.0, 0.0).astype(BF16)

        def tie_body(c, seen):
            s = sc_s[c]
            eq = s == tau
            pc = _dot(lower, jnp.where(eq, 1.0, 0.0).astype(BF16)) + seen
            sc_s[c] = jnp.where(jnp.logical_and(eq, pc > need), -jnp.inf, s)
            return pc[KW - 1:KW, :]

        lax.fori_loop(0, nck, tie_body, jnp.zeros((1, QB), F32))

    m_s[...] = jnp.full(m_s.shape, M_INIT, F32)
    acc_s[...] = jnp.zeros(acc_s.shape, F32)
    last = sc_s.shape[0] - 1

    def produce(c, s_ref):
        rows = pl.ds(pl.multiple_of(jnp.minimum(c, last) * KW, KW), KW)
        s_ref[...] = _dot(cn_ref[0, rows, :], qlt_s[...])

    def consume(c, s_ref):
        cc = jnp.minimum(c, last)
        madd = jnp.where(sc_s[cc] >= tau, 0.0, NEG)
        tile = jnp.clip((j * QB - cc * KW) // QB, 0, bias_ref.shape[0] - 1)
        s = s_ref[...] + (bias_ref[tile] + jnp.concatenate([madd] * H, axis=1))
        m_old = m_s[...]
        m_new = jnp.maximum(m_old, jnp.max(s, axis=0, keepdims=True))
        pr = jnp.exp2(s - m_new)
        m_s[...] = m_new
        acc_s[...] = jnp.exp2(m_old - m_new) * acc_s[...] + _dot(cnt_ref[0, cc], pr.astype(BF16))

    produce(0, sa_s)

    def pair_body(i, _):
        produce(2 * i + 1, sb_s)
        consume(2 * i, sa_s)
        produce(2 * i + 2, sa_s)
        consume(2 * i + 1, sb_s)
        return 0

    lax.fori_loop(0, (nck + 1) // 2, pair_body, 0)

    inv = 1.0 / acc_s[R:R + 1, :]
    octs = []
    for h in range(H):
        hs = slice(h * QB, (h + 1) * QB)
        octs.append(_dot(wuvt_ref[h], (acc_s[0:R, hs] * inv[:, hs]).astype(BF16)))
    o_ref[0] = jnp.concatenate(octs, axis=0).T.astype(o_ref.dtype)


def _dsa(q_c, qi, small, cn, cnt, kin, wuk, wuvt, bias):
    B, T, _ = q_c.shape
    QB, KW = DSA_QB, DSA_KW
    W = C_HEADS * QB
    topk = min(DSA_TOPK_MAX, T // 4)
    assert (T // KW) % DSA_CNT_UNROLL == 0
    row = lambda w: pl.BlockSpec((1, QB, w), lambda b, j: (b, j, 0))
    return pl.pallas_call(
        functools.partial(_dsa_kernel, topk=topk),
        grid=(B, T // QB),
        in_specs=[row(C_WIDTH), row(IDX_HEADS * IDX_DIM), row(SMALL_W),
                  pl.BlockSpec((1, T, KV_RANK), lambda b, j: (b, 0, 0)),
                  pl.BlockSpec((1,) + cnt.shape[1:], lambda b, j: (b, 0, 0, 0)),
                  pl.BlockSpec((1, T, IDX_DIM), lambda b, j: (b, 0, 0)),
                  pl.BlockSpec(wuk.shape, lambda b, j: (0, 0, 0)),
                  pl.BlockSpec(wuvt.shape, lambda b, j: (0, 0, 0)),
                  pl.BlockSpec(bias.shape, lambda b, j: (0, 0, 0))],
        out_specs=row(C_WIDTH),
        out_shape=jax.ShapeDtypeStruct((B, T, C_WIDTH), BF16),
        scratch_shapes=[pltpu.VMEM((T // KW, KW, QB), F32),
                        pltpu.VMEM((KV_RANK + DSA_ACC_PAD, W), F32),
                        pltpu.VMEM((KV_RANK, W), BF16),
                        pltpu.VMEM((1, W), F32),
                        pltpu.VMEM((KW, W), F32),
                        pltpu.VMEM((KW, W), F32)],
        compiler_params=_cparams(("parallel", "arbitrary")),
        name="dsa",
    )(q_c, qi, small, cn, cnt, kin, wuk, wuvt, bias)


def _merge_kernel(x_ref, oa_ref, ob_ref, oc_ref, g_ref, wa_ref, wb_ref, wc_ref, wo_ref, lg_ref, lb_ref,
                  out_ref):
    D = D_MODEL
    y = _sigmoid(g_ref[:, 0:D].astype(F32)) * _dot(oa_ref[...], wa_ref[...])
    y = y + _sigmoid(g_ref[:, D:2 * D].astype(F32)) * _dot(ob_ref[...], wb_ref[...])
    y = y + _sigmoid(g_ref[:, 2 * D:3 * D].astype(F32)) * _dot(oc_ref[...], wc_ref[...])
    mix = _dot(y.astype(BF16), wo_ref[...])
    out_ref[...] = _layer_norm(DEEPNORM_ALPHA * x_ref[...] + mix, lg_ref[...], lb_ref[...])


def _merge(x2, oa, ob, oc, gates, wa, wb, wc, wo, ln_g, ln_b):
    n = x2.shape[0]
    tm = 512
    row = lambda w: pl.BlockSpec((tm, w), lambda i: (i, 0))
    full = lambda a: pl.BlockSpec(a.shape, lambda i: (0, 0))
    lg, lb = ln_g.reshape(1, -1), ln_b.reshape(1, -1)
    return pl.pallas_call(
        _merge_kernel,
        grid=(n // tm,),
        in_specs=[row(D_MODEL), row(A_WIDTH), row(B_WIDTH), row(C_WIDTH), row(3 * D_MODEL),
                  full(wa), full(wb), full(wc), full(wo), full(lg), full(lb)],
        out_specs=row(D_MODEL),
        out_shape=jax.ShapeDtypeStruct((n, D_MODEL), F32),
        compiler_params=_cparams(("parallel",)),
        name="merge",
    )(x2, oa, ob, oc, gates, wa, wb, wc, wo, lg, lb)


def _mlp_kernel(x_ref, wu_ref, wd_ref, lg_ref, lb_ref, out_ref):
    x = x_ref[...]
    xb = x.astype(BF16)
    acc = jnp.zeros(x.shape, F32)
    fc = 1024
    for c in range(D_FF // fc):
        h = jnp.maximum(_dot(xb, wu_ref[:, c * fc:(c + 1) * fc]), 0.0)
        acc = acc + _dot((h * h).astype(BF16), wd_ref[c * fc:(c + 1) * fc, :])
    out_ref[...] = _layer_norm(DEEPNORM_ALPHA * x + acc, lg_ref[...], lb_ref[...])


def _mlp(x2, wu, wd, ln_g, ln_b):
    n = x2.shape[0]
    tm = 512
    lg, lb = ln_g.reshape(1, -1), ln_b.reshape(1, -1)
    return pl.pallas_call(
        _mlp_kernel,
        grid=(n // tm,),
        in_specs=[pl.BlockSpec((tm, D_MODEL), lambda i: (i, 0)),
                  pl.BlockSpec(wu.shape, lambda i: (0, 0), pipeline_mode=pl.Buffered(1)),
                  pl.BlockSpec(wd.shape, lambda i: (0, 0), pipeline_mode=pl.Buffered(1)),
                  pl.BlockSpec(lg.shape, lambda i: (0, 0)),
                  pl.BlockSpec(lb.shape, lambda i: (0, 0))],
        out_specs=pl.BlockSpec((tm, D_MODEL), lambda i: (i, 0)),
        out_shape=jax.ShapeDtypeStruct((n, D_MODEL), F32),
        compiler_params=_cparams(("parallel",)),
        name="mlp",
    )(x2, wu, wd, lg, lb)


def _prep_w_uk(w_uk):
    return (w_uk * (w_uk.shape[-1] ** -0.5 * LOG2E)).transpose(1, 0, 2).astype(BF16)


def _prep_w_uv(w_uv):
    return w_uv.transpose(1, 2, 0).astype(BF16)


def _mixer_branches(x2, B, T, w_pad, conv_w, a_log, dt_bias, gdn_norm, kv_norm, ln_g, ln_b,
                    wuk, wuvt, tab, bias_a, bias_c):
    qkv_a, qkv_b, z_b, q_c, ckv, qi, small, gates = _in_proj(x2, w_pad)

    o_a = _moba(qkv_a.reshape(B, T, -1), tab, bias_a).reshape(B * T, A_WIDTH)

    ab = small[:, SMALL_A:SMALL_A + 2 * B_HEADS].reshape(B, T, 2 * B_HEADS)
    o_b = _gdn(qkv_b.reshape(B, T, -1), z_b.reshape(B, T, -1), ab, ab.transpose(0, 2, 1),
               conv_w, a_log, dt_bias, gdn_norm).reshape(B * T, B_WIDTH)

    small3 = small.reshape(B, T, SMALL_W)
    cn, cnt, kin = _dsa_prep(ckv.reshape(B, T, KV_RANK), small3, kv_norm, ln_g, ln_b)
    o_c = _dsa(q_c.reshape(B, T, -1), qi.reshape(B, T, -1), small3, cn, cnt, kin,
               wuk, wuvt, bias_c).reshape(B * T, C_WIDTH)
    return o_a, o_b, o_c, gates


def kernel(x, rel_bias, w_in, conv_w, a_log, dt_bias, gdn_norm, kv_norm, idx_k_ln_g, idx_k_ln_b, w_uk, w_uv, w_branch_a, w_branch_b, w_branch_c, w_out, ln1_g, ln1_b, w_up, w_down, ln2_g, ln2_b):
    B, T, D = x.shape
    bias_a, bias_c = _bias_tiles(rel_bias)
    tab = rel_bias.T
    x2 = x.reshape(B * T, D)
    for i in range(w_in.shape[0]):
        o_a, o_b, o_c, gates = _mixer_branches(
            x2, B, T, _prep_w_in(w_in[i]), conv_w[i], a_log[i], dt_bias[i], gdn_norm[i], kv_norm[i],
            idx_k_ln_g[i], idx_k_ln_b[i], _prep_w_uk(w_uk[i]), _prep_w_uv(w_uv[i]),
            tab, bias_a, bias_c)
        x2 = _merge(x2, o_a, o_b, o_c, gates, w_branch_a[i].astype(BF16), w_branch_b[i].astype(BF16),
                    w_branch_c[i].astype(BF16), w_out[i].astype(BF16), ln1_g[i], ln1_b[i])
        x2 = _mlp(x2, w_up[i].astype(BF16), w_down[i].astype(BF16), ln2_g[i], ln2_b[i])
    return x2.reshape(B, T, D)
```

```python
import functools
import math

import numpy as np
import jax
import jax.numpy as jnp
from jax import lax
from jax.experimental import pallas as pl
from jax.experimental.pallas import tpu as pltpu

D_MODEL = 1024
DEPTH = 4
A_HEADS = 8
A_HEAD_DIM = 64
A_WIDTH = A_HEADS * A_HEAD_DIM
MOBA_BLOCK = 256
MOBA_TOPK = 3
B_HEADS = 4
B_HEAD_DIM = 128
B_WIDTH = B_HEADS * B_HEAD_DIM
CONV_WIDTH = 4
GDN_CHUNK = 64
C_HEADS = 8
C_HEAD_DIM = 64
C_WIDTH = C_HEADS * C_HEAD_DIM
KV_RANK = 256
IDX_HEADS = 8
IDX_DIM = 64
DSA_TOPK_MAX = 256
IDX_WEIGHT_SCALE = IDX_HEADS ** -0.5 * IDX_DIM ** -0.5
N_BUCKETS = 32
MAX_DISTANCE = 128
D_FF = 4 * D_MODEL
DEEPNORM_ALPHA = (2 * DEPTH) ** 0.25
LN_EPS = 1e-5
RMS_EPS = 1e-6

IN_SPLIT = (A_WIDTH, A_WIDTH, A_WIDTH,
            B_WIDTH, B_WIDTH, B_WIDTH, B_WIDTH, B_HEADS, B_HEADS,
            C_WIDTH, KV_RANK, IDX_HEADS * IDX_DIM, IDX_DIM, IDX_HEADS,
            D_MODEL, D_MODEL, D_MODEL)

F32 = jnp.float32
BF16 = jnp.bfloat16
NEG = -1e9
LOG2E = math.log2(math.e)
MOBA_ACC_PAD = 16
DSA_ACC_PAD = 16
M_INIT = -1e8
LANES = 128
DSA_QB = 128
DSA_KW = 256
DSA_CNT_UNROLL = 4
DSA_BIS_FIXED = 16
DSA_BIS_GROUP = 4
SMALL_W = 128
SMALL_KI = 0
SMALL_WI = IDX_DIM
SMALL_A = SMALL_WI + IDX_HEADS
SMALL_B = SMALL_A + B_HEADS
VMEM_LIMIT = 56 * 1024 * 1024

KEY_LO = int(np.int32(np.uint32(0xFF7FFFFF) ^ np.uint32(0x7FFFFFFF)))
KEY_HI = 0x7F800000


def _cparams(sem):
    return pltpu.CompilerParams(dimension_semantics=sem, vmem_limit_bytes=VMEM_LIMIT)


def _dot(a, b):
    return jnp.dot(a, b, preferred_element_type=F32)


def _dot_nt(a, b):
    return lax.dot_general(a, b, (((1,), (1,)), ((), ())), preferred_element_type=F32)


def _b16(a):
    return a.astype(BF16)


def _bmm(a, b, nt=False):
    dims = (((2,), (2 if nt else 1,)), ((0,), (0,)))
    return lax.dot_general(a, b, dims, preferred_element_type=F32)


def _layer_norm(r, g, b):
    mu = jnp.mean(r, axis=-1, keepdims=True)
    d = r - mu
    var = jnp.mean(d * d, axis=-1, keepdims=True)
    return d * lax.rsqrt(var + LN_EPS) * g + b


def _sigmoid(x):
    return 1.0 / (1.0 + jnp.exp(-x))


def _t5_bucket_np(d):
    max_exact = N_BUCKETS // 2
    d = np.maximum(d, 0)
    large = max_exact + (np.log(np.maximum(d, max_exact).astype(np.float32) / max_exact)
                         / math.log(MAX_DISTANCE / max_exact) * (N_BUCKETS - max_exact)).astype(np.int32)
    return np.where(d < max_exact, d, np.minimum(large, N_BUCKETS - 1))


def _bucket_tiles(qb, w, offs):
    tq = np.arange(qb)[:, None]
    tk = np.arange(w)[None, :]
    tiles = []
    for off in offs:
        d = off + tq - tk
        tiles.append(np.where(d >= 0, _t5_bucket_np(d), -1))
    return np.stack(tiles).astype(np.int32)


def _bias_kernel(tab_ref, mi_ref, di_ref, mo_ref, do_ref):
    h = pl.program_id(0)

    def build(idx, col):
        acc = jnp.full(idx.shape, NEG, F32)
        for b in range(N_BUCKETS):
            acc = jnp.where(idx == b, tab_ref[col, b], acc)
        return acc

    mo_ref[...] = build(mi_ref[...], h) * LOG2E
    do_ref[...] = build(di_ref[...], A_HEADS + h) * LOG2E


def _bias_tiles(rel_bias):
    mi = jnp.asarray(_bucket_tiles(MOBA_BLOCK, MOBA_BLOCK, (0, MOBA_BLOCK)).transpose(0, 2, 1))
    di = jnp.asarray(_bucket_tiles(DSA_QB, DSA_KW, tuple(t * DSA_QB for t in range(4))).transpose(0, 2, 1))
    return pl.pallas_call(
        _bias_kernel,
        grid=(A_HEADS,),
        in_specs=[pl.BlockSpec(memory_space=pltpu.SMEM),
                  pl.BlockSpec(mi.shape, lambda h: (0, 0, 0)),
                  pl.BlockSpec(di.shape, lambda h: (0, 0, 0))],
        out_specs=[pl.BlockSpec(mi.shape, lambda h: (0, 0, h)),
                   pl.BlockSpec(di.shape, lambda h: (0, 0, h))],
        out_shape=[jax.ShapeDtypeStruct(mi.shape[:2] + (A_HEADS * mi.shape[2],), F32),
                   jax.ShapeDtypeStruct(di.shape[:2] + (C_HEADS * di.shape[2],), F32)],
        compiler_params=_cparams(("arbitrary",)),
        name="bias_tiles",
    )(rel_bias.T, mi, di)


IN_SEGS = (("qkv_a", 3 * A_WIDTH, BF16), ("qkv_b", 3 * B_WIDTH, F32), ("z_b", B_WIDTH, F32),
           ("q_c", C_WIDTH, BF16), ("ckv", KV_RANK, F32), ("qi", IDX_HEADS * IDX_DIM, BF16),
           ("small", SMALL_W, F32), ("gates", 3 * D_MODEL, BF16))
IN_PAD_WIDTH = sum(s[1] for s in IN_SEGS)


def _prep_w_in(w):
    cs = np.cumsum((0,) + IN_SPLIT)

    def seg(i):
        return w[:, cs[i]:cs[i + 1]]

    pad = jnp.zeros((w.shape[0], SMALL_W - (IDX_DIM + IDX_HEADS + 2 * B_HEADS)), w.dtype)
    small = jnp.concatenate([seg(12), seg(13), seg(7), seg(8), pad], axis=1)
    return jnp.concatenate([w[:, cs[0]:cs[3]], w[:, cs[3]:cs[6]], seg(6), seg(9), seg(10), seg(11),
                            small, w[:, cs[14]:cs[17]]], axis=1).astype(BF16)


def _in_proj_kernel(x_ref, w_ref, *out_refs):
    xb = x_ref[...].astype(BF16)
    off = 0
    for (_, width, _), o_ref in zip(IN_SEGS, out_refs):
        for c0 in range(0, width, 512):
            cw = min(512, width - c0)
            o_ref[:, c0:c0 + cw] = _dot(xb, w_ref[:, off + c0:off + c0 + cw]).astype(o_ref.dtype)
        off += width


def _in_proj(x2, w_pad):
    n = x2.shape[0]
    tm = 512
    return pl.pallas_call(
        _in_proj_kernel,
        grid=(n // tm,),
        in_specs=[pl.BlockSpec((tm, D_MODEL), lambda i: (i, 0)),
                  pl.BlockSpec((D_MODEL, IN_PAD_WIDTH), lambda i: (0, 0), pipeline_mode=pl.Buffered(1))],
        out_specs=[pl.BlockSpec((tm, wd), lambda i: (i, 0)) for _, wd, _ in IN_SEGS],
        out_shape=[jax.ShapeDtypeStruct((n, wd), dt) for _, wd, dt in IN_SEGS],
        compiler_params=_cparams(("parallel",)),
        name="in_proj",
    )(x2, w_pad)


def _moba_kernel(tab_ref, q_ref, k_ref, v_ref, bias_ref, o_ref,
                 kmean_s, vt_s, qbd_s, cm_s, m_s, acc_s, sa_s, sb_s, *, nb):
    i = pl.program_id(1)
    BS, H, dh = MOBA_BLOCK, A_HEADS, A_HEAD_DIM
    dhp = dh + MOBA_ACC_PAD

    @pl.when(i == 0)
    def _():
        ones_row = jnp.where(lax.broadcasted_iota(jnp.int32, (MOBA_ACC_PAD, BS), 0) == 0, 1.0, 0.0)

        def prep(n, _):
            rows = pl.ds(pl.multiple_of(n * BS, BS), BS)
            kmean_s[pl.ds(n, 1), :] = jnp.mean(k_ref[0, rows, :].astype(F32), axis=0, keepdims=True)
            vt = v_ref[0, rows, :].astype(F32).T
            parts = []
            for h in range(H):
                parts += [vt[h * dh:(h + 1) * dh], ones_row]
            vt_s[n] = jnp.concatenate(parts, axis=0).astype(BF16)
            return 0

        lax.fori_loop(0, nb, prep, 0)

    qt = (q_ref[0].astype(F32) * (dh ** -0.5 * LOG2E)).T.astype(BF16)
    zeros = jnp.zeros((dh, BS), BF16)
    for p in range(H // 2):
        top = jnp.concatenate([qt[2 * p * dh:(2 * p + 1) * dh], zeros], axis=1)
        bot = jnp.concatenate([zeros, qt[(2 * p + 1) * dh:(2 * p + 2) * dh]], axis=1)
        qbd_s[p] = jnp.concatenate([top, bot], axis=0)

    def all_heads(lhs_fn):
        return jnp.concatenate([lhs_fn(p) for p in range(H // 2)], axis=1)

    def gate_pair(p):
        km = kmean_s[:, p * LANES:(p + 1) * LANES]
        km_hi = km.astype(BF16)
        km_lo = (km - km_hi.astype(F32)).astype(BF16)
        return _dot(km_hi, qbd_s[p]) + _dot(km_lo, qbd_s[p])

    W = H * BS
    blk = lax.broadcasted_iota(jnp.int32, (nb, W), 0)
    valid = blk < i
    g = jnp.where(valid, all_heads(gate_pair), -jnp.inf)
    rank = jnp.zeros((nb, W), jnp.int32)
    for m in range(nb):
        gm = g[m:m + 1, :]
        beats = jnp.logical_or(gm > g, jnp.logical_and(gm == g, m < blk))
        rank = rank + jnp.where(beats, 1, 0)
    sel = jnp.logical_and(valid, rank < MOBA_TOPK)
    sat = jnp.concatenate([jnp.full((1, BS), tab_ref[h, N_BUCKETS - 1] * LOG2E, F32) for h in range(H)],
                          axis=1)
    far_bias = jnp.where(blk < i - 1, sat, 0.0)
    cm_s[0:nb, :] = jnp.where(sel, far_bias, jnp.where(blk == i, 0.0, NEG))
    cm_s[nb:nb + 1, :] = jnp.full((1, W), NEG, F32)

    m_s[...] = jnp.full(m_s.shape, M_INIT, F32)
    acc_s[...] = jnp.zeros(acc_s.shape, F32)

    def produce(n, s_ref):
        rows = pl.ds(pl.multiple_of(jnp.minimum(n, nb - 1) * BS, BS), BS)
        s_ref[...] = all_heads(lambda p: _dot(k_ref[0, rows, p * LANES:(p + 1) * LANES], qbd_s[p]))

    def consume(n, crow, s_ref, tile):
        s = s_ref[...]
        if tile is not None:
            s = s + bias_ref[tile]
        c = cm_s[pl.ds(crow, 1), :]
        m_old = m_s[...]
        m_new = jnp.maximum(m_old, jnp.max(s, axis=0, keepdims=True) + c)
        pb = jnp.exp2(s - (m_new - c)).astype(BF16)
        alpha = jnp.exp2(m_old - m_new)
        m_s[...] = m_new
        nn = jnp.minimum(n, nb - 1)
        for h in range(H):
            hs = slice(h * dhp, (h + 1) * dhp)
            qs = slice(h * BS, (h + 1) * BS)
            acc_s[hs, :] = alpha[:, qs] * acc_s[hs, :] + _dot(vt_s[nn, hs, :], pb[:, qs])

    nfar = jnp.maximum(i - 1, 0)
    crow = lambda n: jnp.where(n < nfar, n, nb)
    prev = jnp.maximum(i - 1, 0)
    produce(i, sa_s)
    produce(prev, sb_s)
    consume(i, i, sa_s, 0)
    produce(0, sa_s)
    consume(prev, jnp.where(i >= 1, i - 1, nb), sb_s, 1)

    def pair_body(p, _):
        produce(2 * p + 1, sb_s)
        consume(2 * p, crow(2 * p), sa_s, None)
        produce(2 * p + 2, sa_s)
        consume(2 * p + 1, crow(2 * p + 1), sb_s, None)
        return 0

    lax.fori_loop(0, (nfar + 1) // 2, pair_body, 0)

    parts = []
    for h in range(H):
        inv = 1.0 / acc_s[h * dhp + dh:h * dhp + dh + 1, :]
        parts.append(acc_s[h * dhp:h * dhp + dh, :] * inv)
    o_ref[0] = jnp.concatenate(parts, axis=0).T.astype(o_ref.dtype)


def _moba(qkv, tab, bias):
    B, T, _ = qkv.shape
    BS = MOBA_BLOCK
    nb = T // BS
    W = A_HEADS * BS
    vt_rows = A_HEADS * (A_HEAD_DIM + MOBA_ACC_PAD)
    return pl.pallas_call(
        functools.partial(_moba_kernel, nb=nb),
        grid=(B, nb),
        in_specs=[pl.BlockSpec(memory_space=pltpu.SMEM),
                  pl.BlockSpec((1, BS, A_WIDTH), lambda b, i: (b, i, 0)),
                  pl.BlockSpec((1, T, A_WIDTH), lambda b, i: (b, 0, 1)),
                  pl.BlockSpec((1, T, A_WIDTH), lambda b, i: (b, 0, 2)),
                  pl.BlockSpec(bias.shape, lambda b, i: (0, 0, 0))],
        out_specs=pl.BlockSpec((1, BS, A_WIDTH), lambda b, i: (b, i, 0)),
        out_shape=jax.ShapeDtypeStruct((B, T, A_WIDTH), BF16),
        scratch_shapes=[pltpu.VMEM((nb, A_WIDTH), F32),
                        pltpu.VMEM((nb, vt_rows, BS), BF16),
                        pltpu.VMEM((A_HEADS // 2, 2 * A_HEAD_DIM, 2 * BS), BF16),
                        pltpu.VMEM((nb + 8, W), F32),
                        pltpu.VMEM((1, W), F32),
                        pltpu.VMEM((vt_rows, BS), F32),
                        pltpu.VMEM((BS, W), F32),
                        pltpu.VMEM((BS, W), F32)],
        compiler_params=_cparams(("parallel", "arbitrary")),
        name="moba",
    )(tab, qkv, qkv, qkv, bias)


def _softplus(x):
    return jnp.maximum(x, 0.0) + jnp.log(1.0 + jnp.exp(-jnp.abs(x)))


def _gdn_kernel(x_ref, z_ref, ab_ref, abt_ref, cw_ref, alog_ref, dtb_ref, alogt_ref, dtbt_ref, gn_ref,
                o_ref, xpad_s, state_s, *, tt_rows):
    tt = pl.program_id(1)
    C = GDN_CHUNK
    dk = B_HEAD_DIM
    TT = tt_rows

    @pl.when(tt == 0)
    def _():
        state_s[...] = jnp.zeros_like(state_s)
        xpad_s[0:8, :] = jnp.zeros((8, 3 * B_WIDTH), F32)

    xpad_s[8:8 + TT, :] = x_ref[0]

    ab = ab_ref[0]
    abt = abt_ref[0]
    la_cols = -jnp.exp(alog_ref[...]) * _softplus(ab[:, 0:B_HEADS] + dtb_ref[...])
    be_cols = _sigmoid(ab[:, B_HEADS:2 * B_HEADS])
    la_rows = -jnp.exp(alogt_ref[...]) * _softplus(abt[0:B_HEADS, :] + dtbt_ref[...])

    ri = lax.broadcasted_iota(jnp.int32, (TT, TT), 0)
    ci = lax.broadcasted_iota(jnp.int32, (TT, TT), 1)
    same = (ri // C) == (ci // C)
    causal = jnp.logical_and(same, ci <= ri)
    strict = jnp.logical_and(same, ci < ri)
    low = jnp.where(causal, 1.0, 0.0)
    upp = jnp.where(jnp.logical_and(same, ri <= ci), 1.0, 0.0)
    eye = jnp.where(ci == ri, 1.0, 0.0)
    zpad = jnp.zeros((dk - C, dk), F32)

    def conv_silu(col0):
        acc = jnp.zeros((TT, dk), F32)
        for j in range(CONV_WIDTH):
            r0 = 8 - (CONV_WIDTH - 1) + j
            acc = acc + xpad_s[r0:r0 + TT, col0:col0 + dk] * cw_ref[j:j + 1, col0:col0 + dk]
        return acc * _sigmoid(acc)

    heads = range(B_HEADS)
    q = jnp.stack([conv_silu(h * dk) for h in heads])
    k = jnp.stack([conv_silu(B_WIDTH + h * dk) for h in heads])
    v = jnp.stack([conv_silu(2 * B_WIDTH + h * dk) for h in heads])
    q = q * lax.rsqrt(jnp.sum(q * q, axis=-1, keepdims=True) + RMS_EPS) * (dk ** -0.5)
    k = k * lax.rsqrt(jnp.sum(k * k, axis=-1, keepdims=True) + RMS_EPS)
    la_c = jnp.stack([la_cols[:, h:h + 1] for h in heads])
    be = jnp.stack([be_cols[:, h:h + 1] for h in heads])
    la_r = jnp.stack([la_rows[h:h + 1, :] for h in heads])

    g_col = jnp.sum(low[None] * la_r, axis=2, keepdims=True)
    g_row = jnp.sum(upp[None] * la_c, axis=1, keepdims=True)
    diff = g_col - g_row
    decay = jnp.where(causal[None], jnp.exp(jnp.where(causal[None], diff, 0.0)), 0.0)
    kb = k * be
    mm = jnp.where(strict[None], -_bmm(_b16(kb), _b16(k), nt=True) * decay, 0.0)
    tinv = eye[None] + mm
    pw = _b16(mm)
    for _ in range(int(math.log2(C)) - 1):
        pw = _b16(_bmm(pw, pw))
        tinv = tinv + _bmm(_b16(tinv), pw)
    eg = jnp.exp(g_col)
    ts = _b16(tinv)
    u = _bmm(ts, _b16(v * be))
    w = _bmm(ts, _b16(kb * eg))
    a_intra = _bmm(_b16(q), _b16(k), nt=True) * decay
    qe = q * eg

    S = state_s[...]
    v_new, o_state = [], []
    for c in range(TT // C):
        rs = slice(c * C, (c + 1) * C)
        Ss = _b16(S)
        vn = u[:, rs] - _bmm(_b16(w[:, rs]), Ss)
        o_state.append(_bmm(_b16(qe[:, rs]), Ss))
        g_last = g_col[:, c * C + C - 1:c * C + C, :]
        kd = k[:, rs] * jnp.exp(g_last - g_col[:, rs])
        kdt = jnp.stack([jnp.concatenate([kd[h], zpad], axis=0).T for h in heads])
        vnp = jnp.stack([jnp.concatenate([vn[h], zpad], axis=0) for h in heads])
        S = S * jnp.exp(g_last) + _bmm(_b16(kdt), _b16(vnp))
        v_new.append(vn)
    state_s[...] = S
    o = jnp.concatenate(o_state, axis=1) + _bmm(_b16(a_intra), _b16(jnp.concatenate(v_new, axis=1)))

    on = o * lax.rsqrt(jnp.mean(o * o, axis=-1, keepdims=True) + RMS_EPS) * gn_ref[...]
    for h in heads:
        z = z_ref[0, :, h * dk:(h + 1) * dk]
        o_ref[0, :, h * dk:(h + 1) * dk] = (on[h] * (z * _sigmoid(z))).astype(o_ref.dtype)

    xpad_s[0:8, :] = xpad_s[TT:TT + 8, :]


def _gdn(xqkv, z, ab, abt, conv_w, a_log, dt_bias, gdn_norm):
    B, T, _ = xqkv.shape
    TT = 256
    full = lambda shape: pl.BlockSpec(shape, lambda b, t: (0,) * len(shape))
    return pl.pallas_call(
        functools.partial(_gdn_kernel, tt_rows=TT),
        grid=(B, T // TT),
        in_specs=[pl.BlockSpec((1, TT, 3 * B_WIDTH), lambda b, t: (b, t, 0)),
                  pl.BlockSpec((1, TT, B_WIDTH), lambda b, t: (b, t, 0)),
                  pl.BlockSpec((1, TT, 2 * B_HEADS), lambda b, t: (b, t, 0)),
                  pl.BlockSpec((1, 2 * B_HEADS, TT), lambda b, t: (b, 0, t)),
                  full((CONV_WIDTH, 3 * B_WIDTH)),
                  full((1, B_HEADS)), full((1, B_HEADS)), full((B_HEADS, 1)), full((B_HEADS, 1)),
                  full((1, B_HEAD_DIM))],
        out_specs=pl.BlockSpec((1, TT, B_WIDTH), lambda b, t: (b, t, 0)),
        out_shape=jax.ShapeDtypeStruct((B, T, B_WIDTH), BF16),
        scratch_shapes=[pltpu.VMEM((TT + 8, 3 * B_WIDTH), F32),
                        pltpu.VMEM((B_HEADS, B_HEAD_DIM, B_HEAD_DIM), F32)],
        compiler_params=_cparams(("parallel", "arbitrary")),
        name="gdn",
    )(xqkv, z, ab, abt, conv_w, a_log.reshape(1, -1), dt_bias.reshape(1, -1),
      a_log.reshape(-1, 1), dt_bias.reshape(-1, 1), gdn_norm.reshape(1, -1))


def _dsa_prep_kernel(ckv_ref, small_ref, kvn_ref, lng_ref, lnb_ref, cn_ref, cnt_ref, kin_ref):
    c = ckv_ref[0]
    ms = jnp.mean(c * c, axis=-1, keepdims=True)
    cn = c * lax.rsqrt(ms + RMS_EPS) * kvn_ref[...]
    cn_ref[0] = cn.astype(cn_ref.dtype)
    ones_row = jnp.where(lax.broadcasted_iota(jnp.int32, (DSA_ACC_PAD, DSA_KW), 0) == 0, 1.0, 0.0)
    for r in range(cn.shape[0] // DSA_KW):
        ct = jnp.concatenate([cn[r * DSA_KW:(r + 1) * DSA_KW].T, ones_row], axis=0)
        cnt_ref[0, r] = ct.astype(cnt_ref.dtype)
    ki = small_ref[0, :, SMALL_KI:SMALL_KI + IDX_DIM]
    kin_ref[0] = _layer_norm(ki, lng_ref[...], lnb_ref[...]).astype(kin_ref.dtype)


def _dsa_prep(ckv, small, kv_norm, ln_g, ln_b):
    B, T, _ = ckv.shape
    tm = 1024
    vec = lambda n: pl.BlockSpec((1, n), lambda b, i: (0, 0))
    return pl.pallas_call(
        _dsa_prep_kernel,
        grid=(B, T // tm),
        in_specs=[pl.BlockSpec((1, tm, KV_RANK), lambda b, i: (b, i, 0)),
                  pl.BlockSpec((1, tm, SMALL_W), lambda b, i: (b, i, 0)),
                  vec(KV_RANK), vec(IDX_DIM), vec(IDX_DIM)],
        out_specs=[pl.BlockSpec((1, tm, KV_RANK), lambda b, i: (b, i, 0)),
                   pl.BlockSpec((1, tm // DSA_KW, KV_RANK + DSA_ACC_PAD, DSA_KW), lambda b, i: (b, i, 0, 0)),
                   pl.BlockSpec((1, tm, IDX_DIM), lambda b, i: (b, i, 0))],
        out_shape=[jax.ShapeDtypeStruct((B, T, KV_RANK), BF16),
                   jax.ShapeDtypeStruct((B, T // DSA_KW, KV_RANK + DSA_ACC_PAD, DSA_KW), BF16),
                   jax.ShapeDtypeStruct((B, T, IDX_DIM), BF16)],
        compiler_params=_cparams(("parallel", "parallel")),
        name="dsa_prep",
    )(ckv, small, kv_norm.reshape(1, -1), ln_g.reshape(1, -1), ln_b.reshape(1, -1))


def _key_to_f32(k):
    return lax.bitcast_convert_type(jnp.where(k >= 0, k, k ^ 0x7FFFFFFF), F32)


def _dsa_kernel(qc_ref, qi_ref, small_ref, cn_ref, cnt_ref, kin_ref, wuk_ref, wuvt_ref, bias_ref,
                o_ref, sc_s, acc_s, qlt_s, m_s, sa_s, sb_s, *, topk):
    j = pl.program_id(1)
    QB, KW, H, R, dh, DI = DSA_QB, DSA_KW, C_HEADS, KV_RANK, C_HEAD_DIM, IDX_DIM
    nck = (j * QB + QB - 1) // KW + 1

    qct = qc_ref[0].astype(F32).T.astype(BF16)
    for h in range(H):
        qlt_s[:, h * QB:(h + 1) * QB] = _dot(wuk_ref[h], qct[h * dh:(h + 1) * dh]).astype(BF16)

    qit = qi_ref[0].astype(F32).T.astype(BF16)
    qi_all = jnp.concatenate([qit[h * DI:(h + 1) * DI] for h in range(IDX_HEADS)], axis=1)
    wt = small_ref[0].T
    w_all = jnp.concatenate([wt[SMALL_WI + h:SMALL_WI + h + 1] for h in range(IDX_HEADS)],
                            axis=1) * IDX_WEIGHT_SCALE
    tk0 = lax.broadcasted_iota(jnp.int32, (KW, QB), 0)
    tq = j * QB + lax.broadcasted_iota(jnp.int32, (KW, QB), 1)

    def idx_raw(c):
        rows = pl.ds(pl.multiple_of(c * KW, KW), KW)
        return _dot(kin_ref[0, rows, :], qi_all)

    def idx_store(c, raw):
        s8 = jnp.maximum(raw, 0.0) * w_all
        sc = s8[:, 0:QB]
        for h in range(1, IDX_HEADS):
            sc = sc + s8[:, h * QB:(h + 1) * QB]
        sc_s[c] = jnp.where(tk0 + c * KW <= tq, sc, -jnp.inf)

    def idx_body(p, _):
        raw0, raw1 = idx_raw(2 * p), idx_raw(2 * p + 1)
        idx_store(2 * p, raw0)
        idx_store(2 * p + 1, raw1)
        return 0

    npair = (nck + 1) // 2
    lax.fori_loop(0, npair, idx_body, 0)

    ngrp = (nck + DSA_CNT_UNROLL - 1) // DSA_CNT_UNROLL

    def fill_body(c, _):
        sc_s[c] = jnp.full((KW, QB), -jnp.inf, F32)
        return 0

    lax.fori_loop(2 * npair, ngrp * DSA_CNT_UNROLL, fill_body, 0)

    tcol = j * QB + lax.broadcasted_iota(jnp.int32, (1, QB), 1)
    kk = jnp.minimum(tcol + 1, topk).astype(F32)

    def bis_body(_, carry):
        lo, hi, cge, cgt = carry
        mid = (lo >> 1) + (hi >> 1) + (lo & hi & 1) + ((lo ^ hi) & 1)
        midf = _key_to_f32(mid)

        def cnt_body(gi, acc):
            for u in range(DSA_CNT_UNROLL):
                hit = jnp.where(sc_s[gi * DSA_CNT_UNROLL + u] >= midf, 1.0, 0.0)
                acc = acc + jnp.sum(hit.reshape(KW // 8, 8, QB), axis=0)
            return acc

        acc = lax.fori_loop(0, ngrp, cnt_body, jnp.zeros((8, QB), F32))
        cnt = jnp.sum(acc, axis=0, keepdims=True)
        ok = cnt >= kk
        return (jnp.where(ok, mid, lo), jnp.where(ok, hi, mid - 1),
                jnp.where(ok, cnt, cge), jnp.where(ok, cgt, cnt))

    init = (jnp.full((1, QB), KEY_LO, jnp.int32), jnp.full((1, QB), KEY_HI, jnp.int32),
            (tcol + 1).astype(F32), jnp.zeros((1, QB), F32))
    state = lax.fori_loop(0, DSA_BIS_FIXED, bis_body, init)

    def tail_cond(st):
        return jnp.logical_and(st[0] < 32, st[1] == 0)

    def tail_body(st):
        c = st[2:]
        for _ in range(DSA_BIS_GROUP):
            c = bis_body(0, c)
        done = jnp.min(jnp.where(c[2] == kk, 1.0, 0.0)) > 0.5
        return (st[0] + DSA_BIS_GROUP, done.astype(jnp.int32)) + tuple(c)

    st = lax.while_loop(tail_cond, tail_body, (jnp.int32(DSA_BIS_FIXED), jnp.int32(0)) + tuple(state))
    lo, cge, cgt = st[2], st[4], st[5]
    tau = _key_to_f32(lo)

    @pl.when(jnp.max(cge - kk) > 0)
    def _():
        need = kk - cgt
        lr = lax.broadcasted_iota(jnp.int32, (KW, KW), 0)
        lc = lax.broadcasted_iota(jnp.int32, (KW, KW), 1)
        lower = jnp.where(lc <= lr, 1.0, 0.0).astype(BF16)

        def tie_body(c, seen):
            s = sc_s[c]
            eq = s == tau
            pc = _dot(lower, jnp.where(eq, 1.0, 0.0).astype(BF16)) + seen
            sc_s[c] = jnp.where(jnp.logical_and(eq, pc > need), -jnp.inf, s)
            return pc[KW - 1:KW, :]

        lax.fori_loop(0, nck, tie_body, jnp.zeros((1, QB), F32))

    m_s[...] = jnp.full(m_s.shape, M_INIT, F32)
    acc_s[...] = jnp.zeros(acc_s.shape, F32)
    last = sc_s.shape[0] - 1

    def produce(c, s_ref):
        rows = pl.ds(pl.multiple_of(jnp.minimum(c, last) * KW, KW), KW)
        s_ref[...] = _dot(cn_ref[0, rows, :], qlt_s[...])

    def consume(c, s_ref):
        cc = jnp.minimum(c, last)
        madd = jnp.where(sc_s[cc] >= tau, 0.0, NEG)
        tile = jnp.clip((j * QB - cc * KW) // QB, 0, bias_ref.shape[0] - 1)
        s = s_ref[...] + (bias_ref[tile] + jnp.concatenate([madd] * H, axis=1))
        m_old = m_s[...]
        m_new = jnp.maximum(m_old, jnp.max(s, axis=0, keepdims=True))
        pr = jnp.exp2(s - m_new)
        m_s[...] = m_new
        acc_s[...] = jnp.exp2(m_old - m_new) * acc_s[...] + _dot(cnt_ref[0, cc], pr.astype(BF16))

    produce(0, sa_s)

    def pair_body(i, _):
        produce(2 * i + 1, sb_s)
        consume(2 * i, sa_s)
        produce(2 * i + 2, sa_s)
        consume(2 * i + 1, sb_s)
        return 0

    lax.fori_loop(0, (nck + 1) // 2, pair_body, 0)

    inv = 1.0 / acc_s[R:R + 1, :]
    octs = []
    for h in range(H):
        hs = slice(h * QB, (h + 1) * QB)
        octs.append(_dot(wuvt_ref[h], (acc_s[0:R, hs] * inv[:, hs]).astype(BF16)))
    o_ref[0] = jnp.concatenate(octs, axis=0).T.astype(o_ref.dtype)


def _dsa(q_c, qi, small, cn, cnt, kin, wuk, wuvt, bias):
    B, T, _ = q_c.shape
    QB, KW = DSA_QB, DSA_KW
    W = C_HEADS * QB
    topk = min(DSA_TOPK_MAX, T // 4)
    assert (T // KW) % DSA_CNT_UNROLL == 0
    row = lambda w: pl.BlockSpec((1, QB, w), lambda b, j: (b, j, 0))
    return pl.pallas_call(
        functools.partial(_dsa_kernel, topk=topk),
        grid=(B, T // QB),
        in_specs=[row(C_WIDTH), row(IDX_HEADS * IDX_DIM), row(SMALL_W),
                  pl.BlockSpec((1, T, KV_RANK), lambda b, j: (b, 0, 0)),
                  pl.BlockSpec((1,) + cnt.shape[1:], lambda b, j: (b, 0, 0, 0)),
                  pl.BlockSpec((1, T, IDX_DIM), lambda b, j: (b, 0, 0)),
                  pl.BlockSpec(wuk.shape, lambda b, j: (0, 0, 0)),
                  pl.BlockSpec(wuvt.shape, lambda b, j: (0, 0, 0)),
                  pl.BlockSpec(bias.shape, lambda b, j: (0, 0, 0))],
        out_specs=row(C_WIDTH),
        out_shape=jax.ShapeDtypeStruct((B, T, C_WIDTH), BF16),
        scratch_shapes=[pltpu.VMEM((T // KW, KW, QB), F32),
                        pltpu.VMEM((KV_RANK + DSA_ACC_PAD, W), F32),
                        pltpu.VMEM((KV_RANK, W), BF16),
                        pltpu.VMEM((1, W), F32),
                        pltpu.VMEM((KW, W), F32),
                        pltpu.VMEM((KW, W), F32)],
        compiler_params=_cparams(("parallel", "arbitrary")),
        name="dsa",
    )(q_c, qi, small, cn, cnt, kin, wuk, wuvt, bias)


def _merge_kernel(x_ref, oa_ref, ob_ref, oc_ref, g_ref, wa_ref, wb_ref, wc_ref, wo_ref, lg_ref, lb_ref,
                  out_ref):
    D = D_MODEL
    y = _sigmoid(g_ref[:, 0:D].astype(F32)) * _dot(oa_ref[...], wa_ref[...])
    y = y + _sigmoid(g_ref[:, D:2 * D].astype(F32)) * _dot(ob_ref[...], wb_ref[...])
    y = y + _sigmoid(g_ref[:, 2 * D:3 * D].astype(F32)) * _dot(oc_ref[...], wc_ref[...])
    mix = _dot(y.astype(BF16), wo_ref[...])
    out_ref[...] = _layer_norm(DEEPNORM_ALPHA * x_ref[...] + mix, lg_ref[...], lb_ref[...])


def _merge(x2, oa, ob, oc, gates, wa, wb, wc, wo, ln_g, ln_b):
    n = x2.shape[0]
    tm = 512
    row = lambda w: pl.BlockSpec((tm, w), lambda i: (i, 0))
    full = lambda a: pl.BlockSpec(a.shape, lambda i: (0, 0))
    lg, lb = ln_g.reshape(1, -1), ln_b.reshape(1, -1)
    return pl.pallas_call(
        _merge_kernel,
        grid=(n // tm,),
        in_specs=[row(D_MODEL), row(A_WIDTH), row(B_WIDTH), row(C_WIDTH), row(3 * D_MODEL),
                  full(wa), full(wb), full(wc), full(wo), full(lg), full(lb)],
        out_specs=row(D_MODEL),
        out_shape=jax.ShapeDtypeStruct((n, D_MODEL), F32),
        compiler_params=_cparams(("parallel",)),
        name="merge",
    )(x2, oa, ob, oc, gates, wa, wb, wc, wo, lg, lb)


def _mlp_kernel(x_ref, wu_ref, wd_ref, lg_ref, lb_ref, out_ref):
    x = x_ref[...]
    xb = x.astype(BF16)
    acc = jnp.zeros(x.shape, F32)
    fc = 1024
    for c in range(D_FF // fc):
        h = jnp.maximum(_dot(xb, wu_ref[:, c * fc:(c + 1) * fc]), 0.0)
        acc = acc + _dot((h * h).astype(BF16), wd_ref[c * fc:(c + 1) * fc, :])
    out_ref[...] = _layer_norm(DEEPNORM_ALPHA * x + acc, lg_ref[...], lb_ref[...])


def _mlp(x2, wu, wd, ln_g, ln_b):
    n = x2.shape[0]
    tm = 512
    lg, lb = ln_g.reshape(1, -1), ln_b.reshape(1, -1)
    return pl.pallas_call(
        _mlp_kernel,
        grid=(n // tm,),
        in_specs=[pl.BlockSpec((tm, D_MODEL), lambda i: (i, 0)),
                  pl.BlockSpec(wu.shape, lambda i: (0, 0), pipeline_mode=pl.Buffered(1)),
                  pl.BlockSpec(wd.shape, lambda i: (0, 0), pipeline_mode=pl.Buffered(1)),
                  pl.BlockSpec(lg.shape, lambda i: (0, 0)),
                  pl.BlockSpec(lb.shape, lambda i: (0, 0))],
        out_specs=pl.BlockSpec((tm, D_MODEL), lambda i: (i, 0)),
        out_shape=jax.ShapeDtypeStruct((n, D_MODEL), F32),
        compiler_params=_cparams(("parallel",)),
        name="mlp",
    )(x2, wu, wd, lg, lb)


def _prep_w_uk(w_uk):
    return (w_uk * (w_uk.shape[-1] ** -0.5 * LOG2E)).transpose(1, 0, 2).astype(BF16)


def _prep_w_uv(w_uv):
    return w_uv.transpose(1, 2, 0).astype(BF16)


def _mixer_branches(x2, B, T, w_pad, conv_w, a_log, dt_bias, gdn_norm, kv_norm, ln_g, ln_b,
                    wuk, wuvt, tab, bias_a, bias_c):
    qkv_a, qkv_b, z_b, q_c, ckv, qi, small, gates = _in_proj(x2, w_pad)

    o_a = _moba(qkv_a.reshape(B, T, -1), tab, bias_a).reshape(B * T, A_WIDTH)

    ab = small[:, SMALL_A:SMALL_A + 2 * B_HEADS].reshape(B, T, 2 * B_HEADS)
    o_b = _gdn(qkv_b.reshape(B, T, -1), z_b.reshape(B, T, -1), ab, ab.transpose(0, 2, 1),
               conv_w, a_log, dt_bias, gdn_norm).reshape(B * T, B_WIDTH)

    small3 = small.reshape(B, T, SMALL_W)
    cn, cnt, kin = _dsa_prep(ckv.reshape(B, T, KV_RANK), small3, kv_norm, ln_g, ln_b)
    o_c = _dsa(q_c.reshape(B, T, -1), qi.reshape(B, T, -1), small3, cn, cnt, kin,
               wuk, wuvt, bias_c).reshape(B * T, C_WIDTH)
    return o_a, o_b, o_c, gates


def kernel(x, rel_bias, w_in, conv_w, a_log, dt_bias, gdn_norm, kv_norm, idx_k_ln_g, idx_k_ln_b, w_uk, w_uv, w_branch_a, w_branch_b, w_branch_c, w_out, ln1_g, ln1_b, w_up, w_down, ln2_g, ln2_b):
    B, T, D = x.shape
    bias_a, bias_c = _bias_tiles(rel_bias)
    tab = rel_bias.T
    x2 = x.reshape(B * T, D)
    for i in range(w_in.shape[0]):
        o_a, o_b, o_c, gates = _mixer_branches(
            x2, B, T, _prep_w_in(w_in[i]), conv_w[i], a_log[i], dt_bias[i], gdn_norm[i], kv_norm[i],
            idx_k_ln_g[i], idx_k_ln_b[i], _prep_w_uk(w_uk[i]), _prep_w_uv(w_uv[i]),
            tab, bias_a, bias_c)
        x2 = _merge(x2, o_a, o_b, o_c, gates, w_branch_a[i].astype(BF16), w_branch_b[i].astype(BF16),
                    w_branch_c[i].astype(BF16), w_out[i].astype(BF16), ln1_g[i], ln1_b[i])
        x2 = _mlp(x2, w_up[i].astype(BF16), w_down[i].astype(BF16), ln2_g[i], ln2_b[i])
    return x2.reshape(B, T, D)
```

```python
import functools
import math

import numpy as np
import jax
import jax.numpy as jnp
from jax import lax
from jax.experimental import pallas as pl
from jax.experimental.pallas import tpu as pltpu

D_MODEL = 1024
DEPTH = 4
A_HEADS = 8
A_HEAD_DIM = 64
A_WIDTH = A_HEADS * A_HEAD_DIM
MOBA_BLOCK = 256
MOBA_TOPK = 3
B_HEADS = 4
B_HEAD_DIM = 128
B_WIDTH = B_HEADS * B_HEAD_DIM
CONV_WIDTH = 4
GDN_CHUNK = 64
C_HEADS = 8
C_HEAD_DIM = 64
C_WIDTH = C_HEADS * C_HEAD_DIM
KV_RANK = 256
IDX_HEADS = 8
IDX_DIM = 64
DSA_TOPK_MAX = 256
IDX_WEIGHT_SCALE = IDX_HEADS ** -0.5 * IDX_DIM ** -0.5
N_BUCKETS = 32
MAX_DISTANCE = 128
D_FF = 4 * D_MODEL
DEEPNORM_ALPHA = (2 * DEPTH) ** 0.25
LN_EPS = 1e-5
RMS_EPS = 1e-6

IN_SPLIT = (A_WIDTH, A_WIDTH, A_WIDTH,
            B_WIDTH, B_WIDTH, B_WIDTH, B_WIDTH, B_HEADS, B_HEADS,
            C_WIDTH, KV_RANK, IDX_HEADS * IDX_DIM, IDX_DIM, IDX_HEADS,
            D_MODEL, D_MODEL, D_MODEL)

F32 = jnp.float32
BF16 = jnp.bfloat16
NEG = -1e9
LOG2E = math.log2(math.e)
MOBA_ACC_PAD = 16
DSA_ACC_PAD = 16
M_INIT = -1e8
LANES = 128
DSA_QB = 128
DSA_KW = 256
DSA_CNT_UNROLL = 4
DSA_BIS_GROUP = 4
TOP16 = -65536
SMALL_W = 128
SMALL_KI = 0
SMALL_WI = IDX_DIM
SMALL_A = SMALL_WI + IDX_HEADS
SMALL_B = SMALL_A + B_HEADS
VMEM_LIMIT = 56 * 1024 * 1024

KEY_LO = int(np.int32(np.uint32(0xFF7FFFFF) ^ np.uint32(0x7FFFFFFF)))
KEY_HI = 0x7F800000


def _cparams(sem):
    return pltpu.CompilerParams(dimension_semantics=sem, vmem_limit_bytes=VMEM_LIMIT)


def _dot(a, b):
    return jnp.dot(a, b, preferred_element_type=F32)


def _dot_nt(a, b):
    return lax.dot_general(a, b, (((1,), (1,)), ((), ())), preferred_element_type=F32)


def _b16(a):
    return a.astype(BF16)


def _bmm(a, b, nt=False):
    dims = (((2,), (2 if nt else 1,)), ((0,), (0,)))
    return lax.dot_general(a, b, dims, preferred_element_type=F32)


def _layer_norm(r, g, b):
    mu = jnp.mean(r, axis=-1, keepdims=True)
    d = r - mu
    var = jnp.mean(d * d, axis=-1, keepdims=True)
    return d * lax.rsqrt(var + LN_EPS) * g + b


def _sigmoid(x):
    return 1.0 / (1.0 + jnp.exp(-x))


def _t5_bucket_np(d):
    max_exact = N_BUCKETS // 2
    d = np.maximum(d, 0)
    large = max_exact + (np.log(np.maximum(d, max_exact).astype(np.float32) / max_exact)
                         / math.log(MAX_DISTANCE / max_exact) * (N_BUCKETS - max_exact)).astype(np.int32)
    return np.where(d < max_exact, d, np.minimum(large, N_BUCKETS - 1))


def _bucket_tiles(qb, w, offs):
    tq = np.arange(qb)[:, None]
    tk = np.arange(w)[None, :]
    tiles = []
    for off in offs:
        d = off + tq - tk
        tiles.append(np.where(d >= 0, _t5_bucket_np(d), -1))
    return np.stack(tiles).astype(np.int32)


def _bias_kernel(tab_ref, mi_ref, di_ref, mo_ref, do_ref):
    h = pl.program_id(0)

    def build(idx, col):
        acc = jnp.full(idx.shape, NEG, F32)
        for b in range(N_BUCKETS):
            acc = jnp.where(idx == b, tab_ref[col, b], acc)
        return acc

    mo_ref[...] = build(mi_ref[...], h) * LOG2E
    do_ref[...] = build(di_ref[...], A_HEADS + h) * LOG2E


def _bias_tiles(rel_bias):
    mi = jnp.asarray(_bucket_tiles(MOBA_BLOCK, MOBA_BLOCK, (0, MOBA_BLOCK)).transpose(0, 2, 1))
    di = jnp.asarray(_bucket_tiles(DSA_QB, DSA_KW, tuple(t * DSA_QB for t in range(4))).transpose(0, 2, 1))
    return pl.pallas_call(
        _bias_kernel,
        grid=(A_HEADS,),
        in_specs=[pl.BlockSpec(memory_space=pltpu.SMEM),
                  pl.BlockSpec(mi.shape, lambda h: (0, 0, 0)),
                  pl.BlockSpec(di.shape, lambda h: (0, 0, 0))],
        out_specs=[pl.BlockSpec(mi.shape, lambda h: (0, 0, h)),
                   pl.BlockSpec(di.shape, lambda h: (0, 0, h))],
        out_shape=[jax.ShapeDtypeStruct(mi.shape[:2] + (A_HEADS * mi.shape[2],), F32),
                   jax.ShapeDtypeStruct(di.shape[:2] + (C_HEADS * di.shape[2],), F32)],
        compiler_params=_cparams(("arbitrary",)),
        name="bias_tiles",
    )(rel_bias.T, mi, di)


IN_SEGS = (("qkv_a", 3 * A_WIDTH, BF16), ("qkv_b", 3 * B_WIDTH, F32), ("z_b", B_WIDTH, F32),
           ("q_c", C_WIDTH, BF16), ("ckv", KV_RANK, F32), ("qi", IDX_HEADS * IDX_DIM, BF16),
           ("small", SMALL_W, F32), ("gates", 3 * D_MODEL, BF16))
IN_PAD_WIDTH = sum(s[1] for s in IN_SEGS)


def _prep_w_in(w):
    cs = np.cumsum((0,) + IN_SPLIT)

    def seg(i):
        return w[:, cs[i]:cs[i + 1]]

    pad = jnp.zeros((w.shape[0], SMALL_W - (IDX_DIM + IDX_HEADS + 2 * B_HEADS)), w.dtype)
    small = jnp.concatenate([seg(12), seg(13), seg(7), seg(8), pad], axis=1)
    return jnp.concatenate([w[:, cs[0]:cs[3]], w[:, cs[3]:cs[6]], seg(6), seg(9), seg(10), seg(11),
                            small, w[:, cs[14]:cs[17]]], axis=1).astype(BF16)


def _in_proj_kernel(x_ref, w_ref, *out_refs):
    xb = x_ref[...].astype(BF16)
    off = 0
    for (_, width, _), o_ref in zip(IN_SEGS, out_refs):
        for c0 in range(0, width, 512):
            cw = min(512, width - c0)
            o_ref[:, c0:c0 + cw] = _dot(xb, w_ref[:, off + c0:off + c0 + cw]).astype(o_ref.dtype)
        off += width


def _in_proj(x2, w_pad):
    n = x2.shape[0]
    tm = 512
    return pl.pallas_call(
        _in_proj_kernel,
        grid=(n // tm,),
        in_specs=[pl.BlockSpec((tm, D_MODEL), lambda i: (i, 0)),
                  pl.BlockSpec((D_MODEL, IN_PAD_WIDTH), lambda i: (0, 0), pipeline_mode=pl.Buffered(1))],
        out_specs=[pl.BlockSpec((tm, wd), lambda i: (i, 0)) for _, wd, _ in IN_SEGS],
        out_shape=[jax.ShapeDtypeStruct((n, wd), dt) for _, wd, dt in IN_SEGS],
        compiler_params=_cparams(("parallel",)),
        name="in_proj",
    )(x2, w_pad)


def _moba_kernel(tab_ref, q_ref, k_ref, v_ref, bias_ref, o_ref,
                 kmean_s, vt_s, qbd_s, cm_s, m_s, acc_s, sa_s, sb_s, ma_s, mb_s, *, nb):
    i = pl.program_id(1)
    BS, H, dh = MOBA_BLOCK, A_HEADS, A_HEAD_DIM
    dhp = dh + MOBA_ACC_PAD

    @pl.when(i == 0)
    def _():
        ones_row = jnp.where(lax.broadcasted_iota(jnp.int32, (MOBA_ACC_PAD, BS), 0) == 0, 1.0, 0.0)

        def prep(n, _):
            rows = pl.ds(pl.multiple_of(n * BS, BS), BS)
            kmean_s[pl.ds(n, 1), :] = jnp.mean(k_ref[0, rows, :].astype(F32), axis=0, keepdims=True)
            vt = v_ref[0, rows, :].astype(F32).T
            parts = []
            for h in range(H):
                parts += [vt[h * dh:(h + 1) * dh], ones_row]
            vt_s[n] = jnp.concatenate(parts, axis=0).astype(BF16)
            return 0

        lax.fori_loop(0, nb, prep, 0)

    qt = (q_ref[0].astype(F32) * (dh ** -0.5 * LOG2E)).T.astype(BF16)
    zeros = jnp.zeros((dh, BS), BF16)
    for p in range(H // 2):
        top = jnp.concatenate([qt[2 * p * dh:(2 * p + 1) * dh], zeros], axis=1)
        bot = jnp.concatenate([zeros, qt[(2 * p + 1) * dh:(2 * p + 2) * dh]], axis=1)
        qbd_s[p] = jnp.concatenate([top, bot], axis=0)

    def all_heads(lhs_fn):
        return jnp.concatenate([lhs_fn(p) for p in range(H // 2)], axis=1)

    def gate_pair(p):
        km = kmean_s[:, p * LANES:(p + 1) * LANES]
        km_hi = km.astype(BF16)
        km_lo = (km - km_hi.astype(F32)).astype(BF16)
        return _dot(km_hi, qbd_s[p]) + _dot(km_lo, qbd_s[p])

    W = H * BS
    blk = lax.broadcasted_iota(jnp.int32, (nb, W), 0)
    valid = blk < i
    g = jnp.where(valid, all_heads(gate_pair), -jnp.inf)
    rank = jnp.zeros((nb, W), jnp.int32)
    for m in range(nb):
        gm = g[m:m + 1, :]
        beats = jnp.logical_or(gm > g, jnp.logical_and(gm == g, m < blk))
        rank = rank + jnp.where(beats, 1, 0)
    sel = jnp.logical_and(valid, rank < MOBA_TOPK)
    sat = jnp.concatenate([jnp.full((1, BS), tab_ref[h, N_BUCKETS - 1] * LOG2E, F32) for h in range(H)],
                          axis=1)
    far_bias = jnp.where(blk < i - 1, sat, 0.0)
    cm_s[0:nb, :] = jnp.where(sel, far_bias, jnp.where(blk == i, 0.0, NEG))
    cm_s[nb:nb + 1, :] = jnp.full((1, W), NEG, F32)

    m_s[...] = jnp.full(m_s.shape, M_INIT, F32)
    acc_s[...] = jnp.zeros(acc_s.shape, F32)

    def produce(n, buf, tile):
        s_ref, mx_ref = buf
        rows = pl.ds(pl.multiple_of(jnp.minimum(n, nb - 1) * BS, BS), BS)
        s = all_heads(lambda p: _dot(k_ref[0, rows, p * LANES:(p + 1) * LANES], qbd_s[p]))
        if tile is not None:
            s = s + bias_ref[tile]
        s_ref[...] = s
        mx_ref[...] = jnp.max(s, axis=0, keepdims=True)

    def consume(n, crow, buf):
        s_ref, mx_ref = buf
        c = cm_s[pl.ds(crow, 1), :]
        m_old = m_s[...]
        m_new = jnp.maximum(m_old, mx_ref[...] + c)
        pb = jnp.exp2(s_ref[...] - (m_new - c)).astype(BF16)
        alpha = jnp.exp2(m_old - m_new)
        m_s[...] = m_new
        nn = jnp.minimum(n, nb - 1)
        for h in range(H):
            hs = slice(h * dhp, (h + 1) * dhp)
            qs = slice(h * BS, (h + 1) * BS)
            acc_s[hs, :] = alpha[:, qs] * acc_s[hs, :] + _dot(vt_s[nn, hs, :], pb[:, qs])

    buf_a, buf_b = (sa_s, ma_s), (sb_s, mb_s)
    nfar = jnp.maximum(i - 1, 0)
    crow = lambda n: jnp.where(n < nfar, n, nb)
    prev = jnp.maximum(i - 1, 0)
    produce(i, buf_a, 0)
    produce(prev, buf_b, 1)
    consume(i, i, buf_a)
    produce(0, buf_a, None)
    consume(prev, jnp.where(i >= 1, i - 1, nb), buf_b)

    def pair_body(p, _):
        produce(2 * p + 1, buf_b, None)
        consume(2 * p, crow(2 * p), buf_a)
        produce(2 * p + 2, buf_a, None)
        consume(2 * p + 1, crow(2 * p + 1), buf_b)
        return 0

    lax.fori_loop(0, (nfar + 1) // 2, pair_body, 0)

    parts = []
    for h in range(H):
        inv = 1.0 / acc_s[h * dhp + dh:h * dhp + dh + 1, :]
        parts.append(acc_s[h * dhp:h * dhp + dh, :] * inv)
    o_ref[0] = jnp.concatenate(parts, axis=0).T.astype(o_ref.dtype)


def _moba(qkv, tab, bias):
    B, T, _ = qkv.shape
    BS = MOBA_BLOCK
    nb = T // BS
    W = A_HEADS * BS
    vt_rows = A_HEADS * (A_HEAD_DIM + MOBA_ACC_PAD)
    return pl.pallas_call(
        functools.partial(_moba_kernel, nb=nb),
        grid=(B, nb),
        in_specs=[pl.BlockSpec(memory_space=pltpu.SMEM),
                  pl.BlockSpec((1, BS, A_WIDTH), lambda b, i: (b, i, 0)),
                  pl.BlockSpec((1, T, A_WIDTH), lambda b, i: (b, 0, 1)),
                  pl.BlockSpec((1, T, A_WIDTH), lambda b, i: (b, 0, 2)),
                  pl.BlockSpec(bias.shape, lambda b, i: (0, 0, 0))],
        out_specs=pl.BlockSpec((1, BS, A_WIDTH), lambda b, i: (b, i, 0)),
        out_shape=jax.ShapeDtypeStruct((B, T, A_WIDTH), BF16),
        scratch_shapes=[pltpu.VMEM((nb, A_WIDTH), F32),
                        pltpu.VMEM((nb, vt_rows, BS), BF16),
                        pltpu.VMEM((A_HEADS // 2, 2 * A_HEAD_DIM, 2 * BS), BF16),
                        pltpu.VMEM((nb + 8, W), F32),
                        pltpu.VMEM((1, W), F32),
                        pltpu.VMEM((vt_rows, BS), F32),
                        pltpu.VMEM((BS, W), F32),
                        pltpu.VMEM((BS, W), F32),
                        pltpu.VMEM((1, W), F32),
                        pltpu.VMEM((1, W), F32)],
        compiler_params=_cparams(("parallel", "arbitrary")),
        name="moba",
    )(tab, qkv, qkv, qkv, bias)


def _softplus(x):
    return jnp.maximum(x, 0.0) + jnp.log(1.0 + jnp.exp(-jnp.abs(x)))


def _gdn_kernel(x_ref, z_ref, ab_ref, abt_ref, cw_ref, alog_ref, dtb_ref, alogt_ref, dtbt_ref, gn_ref,
                o_ref, xpad_s, state_s, *, tt_rows):
    tt = pl.program_id(1)
    C = GDN_CHUNK
    dk = B_HEAD_DIM
    TT = tt_rows

    @pl.when(tt == 0)
    def _():
        state_s[...] = jnp.zeros_like(state_s)
        xpad_s[0:8, :] = jnp.zeros((8, 3 * B_WIDTH), F32)

    xpad_s[8:8 + TT, :] = x_ref[0]

    ab = ab_ref[0]
    abt = abt_ref[0]
    la_cols = -jnp.exp(alog_ref[...]) * _softplus(ab[:, 0:B_HEADS] + dtb_ref[...])
    be_cols = _sigmoid(ab[:, B_HEADS:2 * B_HEADS])
    la_rows = -jnp.exp(alogt_ref[...]) * _softplus(abt[0:B_HEADS, :] + dtbt_ref[...])

    ri = lax.broadcasted_iota(jnp.int32, (TT, TT), 0)
    ci = lax.broadcasted_iota(jnp.int32, (TT, TT), 1)
    same = (ri // C) == (ci // C)
    causal = jnp.logical_and(same, ci <= ri)
    strict = jnp.logical_and(same, ci < ri)
    low = jnp.where(causal, 1.0, 0.0)
    upp = jnp.where(jnp.logical_and(same, ri <= ci), 1.0, 0.0)
    eye = jnp.where(ci == ri, 1.0, 0.0)
    zpad = jnp.zeros((dk - C, dk), F32)

    def conv_silu(col0):
        acc = jnp.zeros((TT, dk), F32)
        for j in range(CONV_WIDTH):
            r0 = 8 - (CONV_WIDTH - 1) + j
            acc = acc + xpad_s[r0:r0 + TT, col0:col0 + dk] * cw_ref[j:j + 1, col0:col0 + dk]
        return acc * _sigmoid(acc)

    heads = range(B_HEADS)
    q = jnp.stack([conv_silu(h * dk) for h in heads])
    k = jnp.stack([conv_silu(B_WIDTH + h * dk) for h in heads])
    v = jnp.stack([conv_silu(2 * B_WIDTH + h * dk) for h in heads])
    q = q * lax.rsqrt(jnp.sum(q * q, axis=-1, keepdims=True) + RMS_EPS) * (dk ** -0.5)
    k = k * lax.rsqrt(jnp.sum(k * k, axis=-1, keepdims=True) + RMS_EPS)
    la_c = jnp.stack([la_cols[:, h:h + 1] for h in heads])
    be = jnp.stack([be_cols[:, h:h + 1] for h in heads])
    la_r = jnp.stack([la_rows[h:h + 1, :] for h in heads])

    g_col = jnp.sum(low[None] * la_r, axis=2, keepdims=True)
    g_row = jnp.sum(upp[None] * la_c, axis=1, keepdims=True)
    diff = g_col - g_row
    decay = jnp.where(causal[None], jnp.exp(jnp.where(causal[None], diff, 0.0)), 0.0)
    kb = k * be
    mm = jnp.where(strict[None], -_bmm(_b16(kb), _b16(k), nt=True) * decay, 0.0)
    tinv = eye[None] + mm
    pw = _b16(mm)
    for _ in range(int(math.log2(C)) - 1):
        pw = _b16(_bmm(pw, pw))
        tinv = tinv + _bmm(_b16(tinv), pw)
    eg = jnp.exp(g_col)
    ts = _b16(tinv)
    u = _bmm(ts, _b16(v * be))
    w = _bmm(ts, _b16(kb * eg))
    a_intra = _bmm(_b16(q), _b16(k), nt=True) * decay
    qe = q * eg

    S = state_s[...]
    v_new, o_state = [], []
    for c in range(TT // C):
        rs = slice(c * C, (c + 1) * C)
        Ss = _b16(S)
        vn = u[:, rs] - _bmm(_b16(w[:, rs]), Ss)
        o_state.append(_bmm(_b16(qe[:, rs]), Ss))
        g_last = g_col[:, c * C + C - 1:c * C + C, :]
        kd = k[:, rs] * jnp.exp(g_last - g_col[:, rs])
        kdt = jnp.stack([jnp.concatenate([kd[h], zpad], axis=0).T for h in heads])
        vnp = jnp.stack([jnp.concatenate([vn[h], zpad], axis=0) for h in heads])
        S = S * jnp.exp(g_last) + _bmm(_b16(kdt), _b16(vnp))
        v_new.append(vn)
    state_s[...] = S
    o = jnp.concatenate(o_state, axis=1) + _bmm(_b16(a_intra), _b16(jnp.concatenate(v_new, axis=1)))

    on = o * lax.rsqrt(jnp.mean(o * o, axis=-1, keepdims=True) + RMS_EPS) * gn_ref[...]
    for h in heads:
        z = z_ref[0, :, h * dk:(h + 1) * dk]
        o_ref[0, :, h * dk:(h + 1) * dk] = (on[h] * (z * _sigmoid(z))).astype(o_ref.dtype)

    xpad_s[0:8, :] = xpad_s[TT:TT + 8, :]


def _gdn(xqkv, z, ab, abt, conv_w, a_log, dt_bias, gdn_norm):
    B, T, _ = xqkv.shape
    TT = 256
    full = lambda shape: pl.BlockSpec(shape, lambda b, t: (0,) * len(shape))
    return pl.pallas_call(
        functools.partial(_gdn_kernel, tt_rows=TT),
        grid=(B, T // TT),
        in_specs=[pl.BlockSpec((1, TT, 3 * B_WIDTH), lambda b, t: (b, t, 0)),
                  pl.BlockSpec((1, TT, B_WIDTH), lambda b, t: (b, t, 0)),
                  pl.BlockSpec((1, TT, 2 * B_HEADS), lambda b, t: (b, t, 0)),
                  pl.BlockSpec((1, 2 * B_HEADS, TT), lambda b, t: (b, 0, t)),
                  full((CONV_WIDTH, 3 * B_WIDTH)),
                  full((1, B_HEADS)), full((1, B_HEADS)), full((B_HEADS, 1)), full((B_HEADS, 1)),
                  full((1, B_HEAD_DIM))],
        out_specs=pl.BlockSpec((1, TT, B_WIDTH), lambda b, t: (b, t, 0)),
        out_shape=jax.ShapeDtypeStruct((B, T, B_WIDTH), BF16),
        scratch_shapes=[pltpu.VMEM((TT + 8, 3 * B_WIDTH), F32),
                        pltpu.VMEM((B_HEADS, B_HEAD_DIM, B_HEAD_DIM), F32)],
        compiler_params=_cparams(("parallel", "arbitrary")),
        name="gdn",
    )(xqkv, z, ab, abt, conv_w, a_log.reshape(1, -1), dt_bias.reshape(1, -1),
      a_log.reshape(-1, 1), dt_bias.reshape(-1, 1), gdn_norm.reshape(1, -1))


def _dsa_prep_kernel(ckv_ref, small_ref, kvn_ref, lng_ref, lnb_ref, cn_ref, cnt_ref, kin_ref):
    c = ckv_ref[0]
    ms = jnp.mean(c * c, axis=-1, keepdims=True)
    cn = c * lax.rsqrt(ms + RMS_EPS) * kvn_ref[...]
    cn_ref[0] = cn.astype(cn_ref.dtype)
    ones_row = jnp.where(lax.broadcasted_iota(jnp.int32, (DSA_ACC_PAD, DSA_KW), 0) == 0, 1.0, 0.0)
    for r in range(cn.shape[0] // DSA_KW):
        ct = jnp.concatenate([cn[r * DSA_KW:(r + 1) * DSA_KW].T, ones_row], axis=0)
        cnt_ref[0, r] = ct.astype(cnt_ref.dtype)
    ki = small_ref[0, :, SMALL_KI:SMALL_KI + IDX_DIM]
    kin_ref[0] = _layer_norm(ki, lng_ref[...], lnb_ref[...]).astype(kin_ref.dtype)


def _dsa_prep(ckv, small, kv_norm, ln_g, ln_b):
    B, T, _ = ckv.shape
    tm = 1024
    vec = lambda n: pl.BlockSpec((1, n), lambda b, i: (0, 0))
    return pl.pallas_call(
        _dsa_prep_kernel,
        grid=(B, T // tm),
        in_specs=[pl.BlockSpec((1, tm, KV_RANK), lambda b, i: (b, i, 0)),
                  pl.BlockSpec((1, tm, SMALL_W), lambda b, i: (b, i, 0)),
                  vec(KV_RANK), vec(IDX_DIM), vec(IDX_DIM)],
        out_specs=[pl.BlockSpec((1, tm, KV_RANK), lambda b, i: (b, i, 0)),
                   pl.BlockSpec((1, tm // DSA_KW, KV_RANK + DSA_ACC_PAD, DSA_KW), lambda b, i: (b, i, 0, 0)),
                   pl.BlockSpec((1, tm, IDX_DIM), lambda b, i: (b, i, 0))],
        out_shape=[jax.ShapeDtypeStruct((B, T, KV_RANK), BF16),
                   jax.ShapeDtypeStruct((B, T // DSA_KW, KV_RANK + DSA_ACC_PAD, DSA_KW), BF16),
                   jax.ShapeDtypeStruct((B, T, IDX_DIM), BF16)],
        compiler_params=_cparams(("parallel", "parallel")),
        name="dsa_prep",
    )(ckv, small, kv_norm.reshape(1, -1), ln_g.reshape(1, -1), ln_b.reshape(1, -1))


def _key_bits(k):
    return jnp.where(k >= 0, k, k ^ 0x7FFFFFFF)


def _key_to_f32(k):
    return lax.bitcast_convert_type(_key_bits(k), F32)


def _dsa_kernel(qc_ref, qi_ref, small_ref, cn_ref, cnt_ref, kin_ref, wuk_ref, wuvt_ref, bias_ref,
                o_ref, sc_s, scb_s, acc_s, qlt_s, m_s, sa_s, sb_s, ma_s, mb_s, *, topk):
    j = pl.program_id(1)
    QB, KW, H, R, dh, DI = DSA_QB, DSA_KW, C_HEADS, KV_RANK, C_HEAD_DIM, IDX_DIM
    nck = (j * QB + QB - 1) // KW + 1

    qct = qc_ref[0].astype(F32).T.astype(BF16)
    for h in range(H):
        qlt_s[:, h * QB:(h + 1) * QB] = _dot(wuk_ref[h], qct[h * dh:(h + 1) * dh]).astype(BF16)

    qit = qi_ref[0].astype(F32).T.astype(BF16)
    qi_all = jnp.concatenate([qit[h * DI:(h + 1) * DI] for h in range(IDX_HEADS)], axis=1)
    wt = small_ref[0].T
    w_all = jnp.concatenate([wt[SMALL_WI + h:SMALL_WI + h + 1] for h in range(IDX_HEADS)],
                            axis=1) * IDX_WEIGHT_SCALE
    tk0 = lax.broadcasted_iota(jnp.int32, (KW, QB), 0)
    tq = j * QB + lax.broadcasted_iota(jnp.int32, (KW, QB), 1)

    def idx_raw(c):
        rows = pl.ds(pl.multiple_of(c * KW, KW), KW)
        return _dot(kin_ref[0, rows, :], qi_all)

    def idx_store(c, raw):
        s8 = jnp.maximum(raw, 0.0) * w_all
        sc = s8[:, 0:QB]
        for h in range(1, IDX_HEADS):
            sc = sc + s8[:, h * QB:(h + 1) * QB]
        sc = jnp.where(tk0 + c * KW <= tq, sc, -jnp.inf)
        sc_s[c] = sc
        top = lax.bitcast_convert_type(sc, jnp.int32) & TOP16
        scb_s[c] = lax.bitcast_convert_type(top, F32).astype(BF16)

    def idx_body(p, _):
        raw0, raw1 = idx_raw(2 * p), idx_raw(2 * p + 1)
        idx_store(2 * p, raw0)
        idx_store(2 * p + 1, raw1)
        return 0

    npair = (nck + 1) // 2
    lax.fori_loop(0, npair, idx_body, 0)

    ngrp = (nck + DSA_CNT_UNROLL - 1) // DSA_CNT_UNROLL

    def fill_body(c, _):
        sc_s[c] = jnp.full((KW, QB), -jnp.inf, F32)
        scb_s[c] = jnp.full((KW, QB), -jnp.inf, BF16)
        return 0

    lax.fori_loop(2 * npair, ngrp * DSA_CNT_UNROLL, fill_body, 0)

    tcol = j * QB + lax.broadcasted_iota(jnp.int32, (1, QB), 1)
    kk = jnp.minimum(tcol + 1, topk).astype(F32)

    def midpoint(lo, hi):
        return (lo >> 1) + (hi >> 1) + (lo & hi & 1) + ((lo ^ hi) & 1)

    def step(carry, mid, cnt):
        lo, hi, cge, cgt = carry
        ok = cnt >= kk
        return (jnp.where(ok, mid, lo), jnp.where(ok, hi, mid - 1),
                jnp.where(ok, cnt, cge), jnp.where(ok, cgt, cnt))

    one_b, zero_b = jnp.ones((), BF16), jnp.zeros((), BF16)

    def coarse_body(_, carry):
        mid = midpoint(carry[0], carry[1])
        midb = lax.bitcast_convert_type(_key_bits(mid << 16) & TOP16, F32).astype(BF16)

        def cnt_body(gi, acc):
            for u in range(DSA_CNT_UNROLL):
                hit = jnp.where(scb_s[gi * DSA_CNT_UNROLL + u] >= midb, one_b, zero_b)
                parts = [hit[r * 16:(r + 1) * 16] for r in range(KW // 16)]
                while len(parts) > 1:
                    parts = [parts[i] + parts[i + 1] for i in range(0, len(parts), 2)]
                acc = acc + parts[0].astype(F32)
            return acc

        acc = lax.fori_loop(0, ngrp, cnt_body, jnp.zeros((16, QB), F32))
        return step(carry, mid, jnp.sum(acc, axis=0, keepdims=True))

    def fine_body(_, carry):
        mid = midpoint(carry[0], carry[1])
        midf = _key_to_f32(mid)

        def cnt_body(gi, acc):
            for u in range(DSA_CNT_UNROLL):
                hit = jnp.where(sc_s[gi * DSA_CNT_UNROLL + u] >= midf, 1.0, 0.0)
                acc = acc + jnp.sum(hit.reshape(KW // 8, 8, QB), axis=0)
            return acc

        acc = lax.fori_loop(0, ngrp, cnt_body, jnp.zeros((8, QB), F32))
        return step(carry, mid, jnp.sum(acc, axis=0, keepdims=True))

    init = (jnp.full((1, QB), KEY_LO >> 16, jnp.int32), jnp.full((1, QB), KEY_HI >> 16, jnp.int32),
            (tcol + 1).astype(F32), jnp.zeros((1, QB), F32))
    lo16, _, cge, cgt = lax.fori_loop(0, 16, coarse_body, init)
    state = (lo16 << 16, (lo16 << 16) | 0xFFFF, cge, cgt)

    def all_exact(c):
        return (jnp.min(jnp.where(c[2] == kk, 1.0, 0.0)) > 0.5).astype(jnp.int32)

    def tail_cond(st):
        return jnp.logical_and(st[0] < 16, st[1] == 0)

    def tail_body(st):
        c = st[2:]
        for _ in range(DSA_BIS_GROUP):
            c = fine_body(0, c)
        return (st[0] + DSA_BIS_GROUP, all_exact(c)) + tuple(c)

    st = lax.while_loop(tail_cond, tail_body, (jnp.int32(0), all_exact(state)) + tuple(state))
    lo, cge, cgt = st[2], st[4], st[5]
    tau = _key_to_f32(lo)

    @pl.when(jnp.max(cge - kk) > 0)
    def _():
        need = kk - cgt
        lr = lax.broadcasted_iota(jnp.int32, (KW, KW), 0)
        lc = lax.broadcasted_iota(jnp.int32, (KW, KW), 1)
        lower = jnp.where(lc <= lr, 1.0, 0.0).astype(BF16)

        def tie_body(c, seen):
            s = sc_s[c]
            eq = s == tau
            pc = _dot(lower, jnp.where(eq, 1.0, 0.0).astype(BF16)) + seen
            sc_s[c] = jnp.where(jnp.logical_and(eq, pc > need), -jnp.inf, s)
            return pc[KW - 1:KW, :]

        lax.fori_loop(0, nck, tie_body, jnp.zeros((1, QB), F32))

    m_s[...] = jnp.full(m_s.shape, M_INIT, F32)
    acc_s[...] = jnp.zeros(acc_s.shape, F32)
    last = sc_s.shape[0] - 1

    def produce(c, s_ref, mx_ref):
        cc = jnp.minimum(c, last)
        rows = pl.ds(pl.multiple_of(cc * KW, KW), KW)
        madd = jnp.where(sc_s[cc] >= tau, 0.0, NEG)
        tile = jnp.clip((j * QB - cc * KW) // QB, 0, bias_ref.shape[0] - 1)
        s = _dot(cn_ref[0, rows, :], qlt_s[...]) + (bias_ref[tile] + jnp.concatenate([madd] * H, axis=1))
        s_ref[...] = s
        mx_ref[...] = jnp.max(s, axis=0, keepdims=True)

    def consume(c, s_ref, mx_ref):
        m_old = m_s[...]
        m_new = jnp.maximum(m_old, mx_ref[...])
        pr = jnp.exp2(s_ref[...] - m_new)
        m_s[...] = m_new
        acc_s[...] = (jnp.exp2(m_old - m_new) * acc_s[...]
                      + _dot(cnt_ref[0, jnp.minimum(c, last)], pr.astype(BF16)))

    produce(0, sa_s, ma_s)

    def pair_body(i, _):
        produce(2 * i + 1, sb_s, mb_s)
        consume(2 * i, sa_s, ma_s)
        produce(2 * i + 2, sa_s, ma_s)
        consume(2 * i + 1, sb_s, mb_s)
        return 0

    lax.fori_loop(0, (nck + 1) // 2, pair_body, 0)

    inv = 1.0 / acc_s[R:R + 1, :]
    octs = []
    for h in range(H):
        hs = slice(h * QB, (h + 1) * QB)
        octs.append(_dot(wuvt_ref[h], (acc_s[0:R, hs] * inv[:, hs]).astype(BF16)))
    o_ref[0] = jnp.concatenate(octs, axis=0).T.astype(o_ref.dtype)


def _dsa(q_c, qi, small, cn, cnt, kin, wuk, wuvt, bias):
    B, T, _ = q_c.shape
    QB, KW = DSA_QB, DSA_KW
    W = C_HEADS * QB
    topk = min(DSA_TOPK_MAX, T // 4)
    assert (T // KW) % DSA_CNT_UNROLL == 0
    row = lambda w: pl.BlockSpec((1, QB, w), lambda b, j: (b, j, 0))
    return pl.pallas_call(
        functools.partial(_dsa_kernel, topk=topk),
        grid=(B, T // QB),
        in_specs=[row(C_WIDTH), row(IDX_HEADS * IDX_DIM), row(SMALL_W),
                  pl.BlockSpec((1, T, KV_RANK), lambda b, j: (b, 0, 0)),
                  pl.BlockSpec((1,) + cnt.shape[1:], lambda b, j: (b, 0, 0, 0)),
                  pl.BlockSpec((1, T, IDX_DIM), lambda b, j: (b, 0, 0)),
                  pl.BlockSpec(wuk.shape, lambda b, j: (0, 0, 0)),
                  pl.BlockSpec(wuvt.shape, lambda b, j: (0, 0, 0)),
                  pl.BlockSpec(bias.shape, lambda b, j: (0, 0, 0))],
        out_specs=row(C_WIDTH),
        out_shape=jax.ShapeDtypeStruct((B, T, C_WIDTH), BF16),
        scratch_shapes=[pltpu.VMEM((T // KW, KW, QB), F32),
                        pltpu.VMEM((T // KW, KW, QB), BF16),
                        pltpu.VMEM((KV_RANK + DSA_ACC_PAD, W), F32),
                        pltpu.VMEM((KV_RANK, W), BF16),
                        pltpu.VMEM((1, W), F32),
                        pltpu.VMEM((KW, W), F32),
                        pltpu.VMEM((KW, W), F32),
                        pltpu.VMEM((1, W), F32),
                        pltpu.VMEM((1, W), F32)],
        compiler_params=_cparams(("parallel", "arbitrary")),
        name="dsa",
    )(q_c, qi, small, cn, cnt, kin, wuk, wuvt, bias)


def _merge_kernel(x_ref, oa_ref, ob_ref, oc_ref, g_ref, wa_ref, wb_ref, wc_ref, wo_ref, lg_ref, lb_ref,
                  out_ref):
    D = D_MODEL
    y = _sigmoid(g_ref[:, 0:D].astype(F32)) * _dot(oa_ref[...], wa_ref[...])
    y = y + _sigmoid(g_ref[:, D:2 * D].astype(F32)) * _dot(ob_ref[...], wb_ref[...])
    y = y + _sigmoid(g_ref[:, 2 * D:3 * D].astype(F32)) * _dot(oc_ref[...], wc_ref[...])
    mix = _dot(y.astype(BF16), wo_ref[...])
    out_ref[...] = _layer_norm(DEEPNORM_ALPHA * x_ref[...] + mix, lg_ref[...], lb_ref[...])


def _merge(x2, oa, ob, oc, gates, wa, wb, wc, wo, ln_g, ln_b):
    n = x2.shape[0]
    tm = 512
    row = lambda w: pl.BlockSpec((tm, w), lambda i: (i, 0))
    full = lambda a: pl.BlockSpec(a.shape, lambda i: (0, 0))
    lg, lb = ln_g.reshape(1, -1), ln_b.reshape(1, -1)
    return pl.pallas_call(
        _merge_kernel,
        grid=(n // tm,),
        in_specs=[row(D_MODEL), row(A_WIDTH), row(B_WIDTH), row(C_WIDTH), row(3 * D_MODEL),
                  full(wa), full(wb), full(wc), full(wo), full(lg), full(lb)],
        out_specs=row(D_MODEL),
        out_shape=jax.ShapeDtypeStruct((n, D_MODEL), F32),
        compiler_params=_cparams(("parallel",)),
        name="merge",
    )(x2, oa, ob, oc, gates, wa, wb, wc, wo, lg, lb)


def _mlp_kernel(x_ref, wu_ref, wd_ref, lg_ref, lb_ref, out_ref):
    x = x_ref[...]
    xb = x.astype(BF16)
    acc = jnp.zeros(x.shape, F32)
    fc = 1024
    for c in range(D_FF // fc):
        h = jnp.maximum(_dot(xb, wu_ref[:, c * fc:(c + 1) * fc]), 0.0)
        acc = acc + _dot((h * h).astype(BF16), wd_ref[c * fc:(c + 1) * fc, :])
    out_ref[...] = _layer_norm(DEEPNORM_ALPHA * x + acc, lg_ref[...], lb_ref[...])


def _mlp(x2, wu, wd, ln_g, ln_b):
    n = x2.shape[0]
    tm = 512
    lg, lb = ln_g.reshape(1, -1), ln_b.reshape(1, -1)
    return pl.pallas_call(
        _mlp_kernel,
        grid=(n // tm,),
        in_specs=[pl.BlockSpec((tm, D_MODEL), lambda i: (i, 0)),
                  pl.BlockSpec(wu.shape, lambda i: (0, 0), pipeline_mode=pl.Buffered(1)),
                  pl.BlockSpec(wd.shape, lambda i: (0, 0), pipeline_mode=pl.Buffered(1)),
                  pl.BlockSpec(lg.shape, lambda i: (0, 0)),
                  pl.BlockSpec(lb.shape, lambda i: (0, 0))],
        out_specs=pl.BlockSpec((tm, D_MODEL), lambda i: (i, 0)),
        out_shape=jax.ShapeDtypeStruct((n, D_MODEL), F32),
        compiler_params=_cparams(("parallel",)),
        name="mlp",
    )(x2, wu, wd, lg, lb)


def _prep_w_uk(w_uk):
    return (w_uk * (w_uk.shape[-1] ** -0.5 * LOG2E)).transpose(1, 0, 2).astype(BF16)


def _prep_w_uv(w_uv):
    return w_uv.transpose(1, 2, 0).astype(BF16)


def _mixer_branches(x2, B, T, w_pad, conv_w, a_log, dt_bias, gdn_norm, kv_norm, ln_g, ln_b,
                    wuk, wuvt, tab, bias_a, bias_c):
    qkv_a, qkv_b, z_b, q_c, ckv, qi, small, gates = _in_proj(x2, w_pad)

    o_a = _moba(qkv_a.reshape(B, T, -1), tab, bias_a).reshape(B * T, A_WIDTH)

    ab = small[:, SMALL_A:SMALL_A + 2 * B_HEADS].reshape(B, T, 2 * B_HEADS)
    o_b = _gdn(qkv_b.reshape(B, T, -1), z_b.reshape(B, T, -1), ab, ab.transpose(0, 2, 1),
               conv_w, a_log, dt_bias, gdn_norm).reshape(B * T, B_WIDTH)

    small3 = small.reshape(B, T, SMALL_W)
    cn, cnt, kin = _dsa_prep(ckv.reshape(B, T, KV_RANK), small3, kv_norm, ln_g, ln_b)
    o_c = _dsa(q_c.reshape(B, T, -1), qi.reshape(B, T, -1), small3, cn, cnt, kin,
               wuk, wuvt, bias_c).reshape(B * T, C_WIDTH)
    return o_a, o_b, o_c, gates


def kernel(x, rel_bias, w_in, conv_w, a_log, dt_bias, gdn_norm, kv_norm, idx_k_ln_g, idx_k_ln_b, w_uk, w_uv, w_branch_a, w_branch_b, w_branch_c, w_out, ln1_g, ln1_b, w_up, w_down, ln2_g, ln2_b):
    B, T, D = x.shape
    bias_a, bias_c = _bias_tiles(rel_bias)
    tab = rel_bias.T
    x2 = x.reshape(B * T, D)
    for i in range(w_in.shape[0]):
        o_a, o_b, o_c, gates = _mixer_branches(
            x2, B, T, _prep_w_in(w_in[i]), conv_w[i], a_log[i], dt_bias[i], gdn_norm[i], kv_norm[i],
            idx_k_ln_g[i], idx_k_ln_b[i], _prep_w_uk(w_uk[i]), _prep_w_uv(w_uv[i]),
            tab, bias_a, bias_c)
        x2 = _merge(x2, o_a, o_b, o_c, gates, w_branch_a[i].astype(BF16), w_branch_b[i].astype(BF16),
                    w_branch_c[i].astype(BF16), w_out[i].astype(BF16), ln1_g[i], ln1_b[i])
        x2 = _mlp(x2, w_up[i].astype(BF16), w_down[i].astype(BF16), ln2_g[i], ln2_b[i])
    return x2.reshape(B, T, D)
```

```python
import functools
import math

import numpy as np
import jax
import jax.numpy as jnp
from jax import lax
from jax.experimental import pallas as pl
from jax.experimental.pallas import tpu as pltpu

D_MODEL = 1024
DEPTH = 4
A_HEADS = 8
A_HEAD_DIM = 64
A_WIDTH = A_HEADS * A_HEAD_DIM
MOBA_BLOCK = 256
MOBA_TOPK = 3
B_HEADS = 4
B_HEAD_DIM = 128
B_WIDTH = B_HEADS * B_HEAD_DIM
CONV_WIDTH = 4
GDN_CHUNK = 64
C_HEADS = 8
C_HEAD_DIM = 64
C_WIDTH = C_HEADS * C_HEAD_DIM
KV_RANK = 256
IDX_HEADS = 8
IDX_DIM = 64
DSA_TOPK_MAX = 256
IDX_WEIGHT_SCALE = IDX_HEADS ** -0.5 * IDX_DIM ** -0.5
N_BUCKETS = 32
MAX_DISTANCE = 128
D_FF = 4 * D_MODEL
DEEPNORM_ALPHA = (2 * DEPTH) ** 0.25
LN_EPS = 1e-5
RMS_EPS = 1e-6

IN_SPLIT = (A_WIDTH, A_WIDTH, A_WIDTH,
            B_WIDTH, B_WIDTH, B_WIDTH, B_WIDTH, B_HEADS, B_HEADS,
            C_WIDTH, KV_RANK, IDX_HEADS * IDX_DIM, IDX_DIM, IDX_HEADS,
            D_MODEL, D_MODEL, D_MODEL)

F32 = jnp.float32
BF16 = jnp.bfloat16
NEG = -1e9
LOG2E = math.log2(math.e)
DSA_ACC_PAD = 16
MOBA_ACC_PAD = 16
M_INIT = -1e8
LANES = 128
DSA_QB = 128
DSA_KW = 256
DSA_CNT_UNROLL = 4
DSA_BIS_GROUP = 4
TOP16 = -65536
SMALL_W = 128
SMALL_KI = 0
SMALL_WI = IDX_DIM
SMALL_A = SMALL_WI + IDX_HEADS
SMALL_B = SMALL_A + B_HEADS
VMEM_LIMIT = 56 * 1024 * 1024

KEY_LO = int(np.int32(np.uint32(0xFF7FFFFF) ^ np.uint32(0x7FFFFFFF)))
KEY_HI = 0x7F800000


def _cparams(sem):
    return pltpu.CompilerParams(dimension_semantics=sem, vmem_limit_bytes=VMEM_LIMIT)


def _dot(a, b):
    return jnp.dot(a, b, preferred_element_type=F32)


def _dot_nt(a, b):
    return lax.dot_general(a, b, (((1,), (1,)), ((), ())), preferred_element_type=F32)


def _b16(a):
    return a.astype(BF16)


def _bmm(a, b, nt=False):
    dims = (((2,), (2 if nt else 1,)), ((0,), (0,)))
    return lax.dot_general(a, b, dims, preferred_element_type=F32)


def _layer_norm(r, g, b):
    mu = jnp.mean(r, axis=-1, keepdims=True)
    d = r - mu
    var = jnp.mean(d * d, axis=-1, keepdims=True)
    return d * lax.rsqrt(var + LN_EPS) * g + b


def _sigmoid(x):
    return 1.0 / (1.0 + jnp.exp(-x))


def _t5_bucket_np(d):
    max_exact = N_BUCKETS // 2
    d = np.maximum(d, 0)
    large = max_exact + (np.log(np.maximum(d, max_exact).astype(np.float32) / max_exact)
                         / math.log(MAX_DISTANCE / max_exact) * (N_BUCKETS - max_exact)).astype(np.int32)
    return np.where(d < max_exact, d, np.minimum(large, N_BUCKETS - 1))


def _bucket_tiles(qb, w, offs):
    tq = np.arange(qb)[:, None]
    tk = np.arange(w)[None, :]
    tiles = []
    for off in offs:
        d = off + tq - tk
        tiles.append(np.where(d >= 0, _t5_bucket_np(d), -1))
    return np.stack(tiles).astype(np.int32)


def _bias_kernel(tab_ref, mi_ref, di_ref, mo_ref, do_ref):
    h = pl.program_id(0)

    def build(idx, col):
        acc = jnp.full(idx.shape, NEG, F32)
        for b in range(N_BUCKETS):
            acc = jnp.where(idx == b, tab_ref[col, b], acc)
        return acc

    mo_ref[...] = build(mi_ref[...], h) * LOG2E
    do_ref[...] = build(di_ref[...], A_HEADS + h) * LOG2E


def _bias_tiles(rel_bias):
    mi = jnp.asarray(_bucket_tiles(MOBA_BLOCK, MOBA_BLOCK, (0, MOBA_BLOCK)).transpose(0, 2, 1))
    di = jnp.asarray(_bucket_tiles(DSA_QB, DSA_KW, tuple(t * DSA_QB for t in range(4))).transpose(0, 2, 1))
    return pl.pallas_call(
        _bias_kernel,
        grid=(A_HEADS,),
        in_specs=[pl.BlockSpec(memory_space=pltpu.SMEM),
                  pl.BlockSpec(mi.shape, lambda h: (0, 0, 0)),
                  pl.BlockSpec(di.shape, lambda h: (0, 0, 0))],
        out_specs=[pl.BlockSpec(mi.shape, lambda h: (0, 0, h)),
                   pl.BlockSpec(di.shape, lambda h: (0, 0, h))],
        out_shape=[jax.ShapeDtypeStruct(mi.shape[:2] + (A_HEADS * mi.shape[2],), F32),
                   jax.ShapeDtypeStruct(di.shape[:2] + (C_HEADS * di.shape[2],), F32)],
        compiler_params=_cparams(("arbitrary",)),
        name="bias_tiles",
    )(rel_bias.T, mi, di)


IN_SEGS = (("qkv_a", 3 * A_WIDTH, BF16), ("qkv_b", 3 * B_WIDTH, F32), ("z_b", B_WIDTH, F32),
           ("q_c", C_WIDTH, BF16), ("ckv", KV_RANK, F32), ("qi", IDX_HEADS * IDX_DIM, BF16),
           ("small", SMALL_W, F32), ("gates", 3 * D_MODEL, BF16))
IN_PAD_WIDTH = sum(s[1] for s in IN_SEGS)


def _prep_w_in(w):
    cs = np.cumsum((0,) + IN_SPLIT)

    def seg(i):
        return w[:, cs[i]:cs[i + 1]]

    pad = jnp.zeros((w.shape[0], SMALL_W - (IDX_DIM + IDX_HEADS + 2 * B_HEADS)), w.dtype)
    small = jnp.concatenate([seg(12), seg(13), seg(7), seg(8), pad], axis=1)
    return jnp.concatenate([w[:, cs[0]:cs[3]], w[:, cs[3]:cs[6]], seg(6), seg(9), seg(10), seg(11),
                            small, w[:, cs[14]:cs[17]]], axis=1).astype(BF16)


def _in_proj_kernel(x_ref, w_ref, *out_refs):
    xb = x_ref[...].astype(BF16)
    off = 0
    for (_, width, _), o_ref in zip(IN_SEGS, out_refs):
        for c0 in range(0, width, 512):
            cw = min(512, width - c0)
            o_ref[:, c0:c0 + cw] = _dot(xb, w_ref[:, off + c0:off + c0 + cw]).astype(o_ref.dtype)
        off += width


def _in_proj(x2, w_pad):
    n = x2.shape[0]
    tm = 512
    return pl.pallas_call(
        _in_proj_kernel,
        grid=(n // tm,),
        in_specs=[pl.BlockSpec((tm, D_MODEL), lambda i: (i, 0)),
                  pl.BlockSpec((D_MODEL, IN_PAD_WIDTH), lambda i: (0, 0), pipeline_mode=pl.Buffered(1))],
        out_specs=[pl.BlockSpec((tm, wd), lambda i: (i, 0)) for _, wd, _ in IN_SEGS],
        out_shape=[jax.ShapeDtypeStruct((n, wd), dt) for _, wd, dt in IN_SEGS],
        compiler_params=_cparams(("parallel",)),
        name="in_proj",
    )(x2, w_pad)


def _moba_kernel(tab_ref, q_ref, k_ref, v_ref, bias_ref, o_ref,
                 kmean_s, vt_s, qbd_s, cm_s, m_s, acc_s, sa_s, sb_s, ma_s, mb_s, *, nb):
    i = pl.program_id(1)
    BS, H, dh = MOBA_BLOCK, A_HEADS, A_HEAD_DIM
    dhp = dh + MOBA_ACC_PAD

    @pl.when(i == 0)
    def _():
        ones_row = jnp.where(lax.broadcasted_iota(jnp.int32, (MOBA_ACC_PAD, BS), 0) == 0, 1.0, 0.0)

        def prep(n, _):
            rows = pl.ds(pl.multiple_of(n * BS, BS), BS)
            kmean_s[pl.ds(n, 1), :] = jnp.mean(k_ref[0, rows, :].astype(F32), axis=0, keepdims=True)
            vt = v_ref[0, rows, :].astype(F32).T
            parts = []
            for h in range(H):
                parts += [vt[h * dh:(h + 1) * dh], ones_row]
            vt_s[n] = jnp.concatenate(parts, axis=0).astype(BF16)
            return 0

        lax.fori_loop(0, nb, prep, 0)

    qt = (q_ref[0].astype(F32) * (dh ** -0.5 * LOG2E)).T.astype(BF16)
    zeros = jnp.zeros((dh, BS), BF16)
    for p in range(H // 2):
        top = jnp.concatenate([qt[2 * p * dh:(2 * p + 1) * dh], zeros], axis=1)
        bot = jnp.concatenate([zeros, qt[(2 * p + 1) * dh:(2 * p + 2) * dh]], axis=1)
        qbd_s[p] = jnp.concatenate([top, bot], axis=0)

    def all_heads(lhs_fn):
        return jnp.concatenate([lhs_fn(p) for p in range(H // 2)], axis=1)

    def gate_pair(p):
        km = kmean_s[:, p * LANES:(p + 1) * LANES]
        km_hi = km.astype(BF16)
        km_lo = (km - km_hi.astype(F32)).astype(BF16)
        return _dot(km_hi, qbd_s[p]) + _dot(km_lo, qbd_s[p])

    W = H * BS
    blk = lax.broadcasted_iota(jnp.int32, (nb, W), 0)
    valid = blk < i
    g = jnp.where(valid, all_heads(gate_pair), -jnp.inf)
    rank = jnp.zeros((nb, W), jnp.int32)
    for m in range(nb):
        gm = g[m:m + 1, :]
        beats = jnp.logical_or(gm > g, jnp.logical_and(gm == g, m < blk))
        rank = rank + jnp.where(beats, 1, 0)
    sel = jnp.logical_and(valid, rank < MOBA_TOPK)
    sat = jnp.concatenate([jnp.full((1, BS), tab_ref[h, N_BUCKETS - 1] * LOG2E, F32) for h in range(H)],
                          axis=1)
    far_bias = jnp.where(blk < i - 1, sat, 0.0)
    cm_s[0:nb, :] = jnp.where(sel, far_bias, jnp.where(blk == i, 0.0, NEG))
    cm_s[nb:nb + 1, :] = jnp.full((1, W), NEG, F32)

    m_s[...] = jnp.full(m_s.shape, M_INIT, F32)
    acc_s[...] = jnp.zeros(acc_s.shape, F32)

    def produce(n, buf, tile):
        s_ref, mx_ref = buf
        rows = pl.ds(pl.multiple_of(jnp.minimum(n, nb - 1) * BS, BS), BS)
        s = all_heads(lambda p: _dot(k_ref[0, rows, p * LANES:(p + 1) * LANES], qbd_s[p]))
        if tile is not None:
            s = s + bias_ref[tile]
        s_ref[...] = s
        mx_ref[...] = jnp.max(s, axis=0, keepdims=True)

    def consume(n, crow, buf):
        s_ref, mx_ref = buf
        c = cm_s[pl.ds(crow, 1), :]
        m_old = m_s[...]
        m_new = jnp.maximum(m_old, mx_ref[...] + c)
        pb = jnp.exp2(s_ref[...] - (m_new - c)).astype(BF16)
        alpha = jnp.exp2(m_old - m_new)
        m_s[...] = m_new
        nn = jnp.minimum(n, nb - 1)
        for h in range(H):
            hs = slice(h * dhp, (h + 1) * dhp)
            qs = slice(h * BS, (h + 1) * BS)
            acc_s[hs, :] = alpha[:, qs] * acc_s[hs, :] + _dot(vt_s[nn, hs, :], pb[:, qs])

    buf_a, buf_b = (sa_s, ma_s), (sb_s, mb_s)
    nfar = jnp.maximum(i - 1, 0)
    crow = lambda n: jnp.where(n < nfar, n, nb)
    prev = jnp.maximum(i - 1, 0)
    produce(i, buf_a, 0)
    produce(prev, buf_b, 1)
    consume(i, i, buf_a)
    produce(0, buf_a, None)
    consume(prev, jnp.where(i >= 1, i - 1, nb), buf_b)

    def pair_body(p, _):
        produce(2 * p + 1, buf_b, None)
        consume(2 * p, crow(2 * p), buf_a)
        produce(2 * p + 2, buf_a, None)
        consume(2 * p + 1, crow(2 * p + 1), buf_b)
        return 0

    tail = (nfar + 1) // 2 - 1
    lax.fori_loop(0, tail, pair_body, 0)

    @pl.when(nfar >= 1)
    def _():
        produce(2 * tail + 1, buf_b, None)
        consume(2 * tail, crow(2 * tail), buf_a)
        consume(2 * tail + 1, crow(2 * tail + 1), buf_b)

    parts = []
    for h in range(H):
        inv = 1.0 / acc_s[h * dhp + dh:h * dhp + dh + 1, :]
        parts.append(acc_s[h * dhp:h * dhp + dh, :] * inv)
    o_ref[0] = jnp.concatenate(parts, axis=0).T.astype(o_ref.dtype)


def _moba(qkv, tab, bias):
    B, T, _ = qkv.shape
    BS = MOBA_BLOCK
    nb = T // BS
    W = A_HEADS * BS
    vt_rows = A_HEADS * (A_HEAD_DIM + MOBA_ACC_PAD)
    return pl.pallas_call(
        functools.partial(_moba_kernel, nb=nb),
        grid=(B, nb),
        in_specs=[pl.BlockSpec(memory_space=pltpu.SMEM),
                  pl.BlockSpec((1, BS, A_WIDTH), lambda b, i: (b, i, 0)),
                  pl.BlockSpec((1, T, A_WIDTH), lambda b, i: (b, 0, 1)),
                  pl.BlockSpec((1, T, A_WIDTH), lambda b, i: (b, 0, 2)),
                  pl.BlockSpec(bias.shape, lambda b, i: (0, 0, 0))],
        out_specs=pl.BlockSpec((1, BS, A_WIDTH), lambda b, i: (b, i, 0)),
        out_shape=jax.ShapeDtypeStruct((B, T, A_WIDTH), BF16),
        scratch_shapes=[pltpu.VMEM((nb, A_WIDTH), F32),
                        pltpu.VMEM((nb, vt_rows, BS), BF16),
                        pltpu.VMEM((A_HEADS // 2, 2 * A_HEAD_DIM, 2 * BS), BF16),
                        pltpu.VMEM((nb + 8, W), F32),
                        pltpu.VMEM((1, W), F32),
                        pltpu.VMEM((vt_rows, BS), F32),
                        pltpu.VMEM((BS, W), F32),
                        pltpu.VMEM((BS, W), F32),
                        pltpu.VMEM((1, W), F32),
                        pltpu.VMEM((1, W), F32)],
        compiler_params=_cparams(("parallel", "arbitrary")),
        name="moba",
    )(tab, qkv, qkv, qkv, bias)


def _softplus(x):
    return jnp.maximum(x, 0.0) + jnp.log(1.0 + jnp.exp(-jnp.abs(x)))


def _gdn_kernel(x_ref, z_ref, ab_ref, abt_ref, cw_ref, alog_ref, dtb_ref, alogt_ref, dtbt_ref, gn_ref,
                o_ref, xpad_s, state_s, *, tt_rows):
    tt = pl.program_id(1)
    C = GDN_CHUNK
    dk = B_HEAD_DIM
    TT = tt_rows

    @pl.when(tt == 0)
    def _():
        state_s[...] = jnp.zeros_like(state_s)
        xpad_s[0:8, :] = jnp.zeros((8, 3 * B_WIDTH), F32)

    xpad_s[8:8 + TT, :] = x_ref[0]

    ab = ab_ref[0]
    abt = abt_ref[0]
    la_cols = -jnp.exp(alog_ref[...]) * _softplus(ab[:, 0:B_HEADS] + dtb_ref[...])
    be_cols = _sigmoid(ab[:, B_HEADS:2 * B_HEADS])
    la_rows = -jnp.exp(alogt_ref[...]) * _softplus(abt[0:B_HEADS, :] + dtbt_ref[...])

    ri = lax.broadcasted_iota(jnp.int32, (TT, TT), 0)
    ci = lax.broadcasted_iota(jnp.int32, (TT, TT), 1)
    same = (ri // C) == (ci // C)
    causal = jnp.logical_and(same, ci <= ri)
    strict = jnp.logical_and(same, ci < ri)
    low = jnp.where(causal, 1.0, 0.0)
    upp = jnp.where(jnp.logical_and(same, ri <= ci), 1.0, 0.0)
    eye = jnp.where(ci == ri, 1.0, 0.0)
    zpad = jnp.zeros((dk - C, dk), F32)

    def conv_silu(col0):
        acc = jnp.zeros((TT, dk), F32)
        for j in range(CONV_WIDTH):
            r0 = 8 - (CONV_WIDTH - 1) + j
            acc = acc + xpad_s[r0:r0 + TT, col0:col0 + dk] * cw_ref[j:j + 1, col0:col0 + dk]
        return acc * _sigmoid(acc)

    heads = range(B_HEADS)
    q = jnp.stack([conv_silu(h * dk) for h in heads])
    k = jnp.stack([conv_silu(B_WIDTH + h * dk) for h in heads])
    v = jnp.stack([conv_silu(2 * B_WIDTH + h * dk) for h in heads])
    q = q * lax.rsqrt(jnp.sum(q * q, axis=-1, keepdims=True) + RMS_EPS) * (dk ** -0.5)
    k = k * lax.rsqrt(jnp.sum(k * k, axis=-1, keepdims=True) + RMS_EPS)
    la_c = jnp.stack([la_cols[:, h:h + 1] for h in heads])
    be = jnp.stack([be_cols[:, h:h + 1] for h in heads])
    la_r = jnp.stack([la_rows[h:h + 1, :] for h in heads])

    g_col = jnp.sum(low[None] * la_r, axis=2, keepdims=True)
    g_row = jnp.sum(upp[None] * la_c, axis=1, keepdims=True)
    diff = g_col - g_row
    decay = jnp.where(causal[None], jnp.exp(jnp.where(causal[None], diff, 0.0)), 0.0)
    kb = k * be
    mm = jnp.where(strict[None], -_bmm(_b16(kb), _b16(k), nt=True) * decay, 0.0)
    tinv = eye[None] + mm
    pw = _b16(mm)
    for _ in range(int(math.log2(C)) - 1):
        pw = _b16(_bmm(pw, pw))
        tinv = tinv + _bmm(_b16(tinv), pw)
    eg = jnp.exp(g_col)
    ts = _b16(tinv)
    u = _bmm(ts, _b16(v * be))
    w = _bmm(ts, _b16(kb * eg))
    a_intra = _bmm(_b16(q), _b16(k), nt=True) * decay
    qe = q * eg

    S = state_s[...]
    v_new, o_state = [], []
    for c in range(TT // C):
        rs = slice(c * C, (c + 1) * C)
        Ss = _b16(S)
        vn = u[:, rs] - _bmm(_b16(w[:, rs]), Ss)
        o_state.append(_bmm(_b16(qe[:, rs]), Ss))
        g_last = g_col[:, c * C + C - 1:c * C + C, :]
        kd = k[:, rs] * jnp.exp(g_last - g_col[:, rs])
        kdt = jnp.stack([jnp.concatenate([kd[h], zpad], axis=0).T for h in heads])
        vnp = jnp.stack([jnp.concatenate([vn[h], zpad], axis=0) for h in heads])
        S = S * jnp.exp(g_last) + _bmm(_b16(kdt), _b16(vnp))
        v_new.append(vn)
    state_s[...] = S
    o = jnp.concatenate(o_state, axis=1) + _bmm(_b16(a_intra), _b16(jnp.concatenate(v_new, axis=1)))

    on = o * lax.rsqrt(jnp.mean(o * o, axis=-1, keepdims=True) + RMS_EPS) * gn_ref[...]
    for h in heads:
        z = z_ref[0, :, h * dk:(h + 1) * dk]
        o_ref[0, :, h * dk:(h + 1) * dk] = (on[h] * (z * _sigmoid(z))).astype(o_ref.dtype)

    xpad_s[0:8, :] = xpad_s[TT:TT + 8, :]


def _gdn(xqkv, z, ab, abt, conv_w, a_log, dt_bias, gdn_norm):
    B, T, _ = xqkv.shape
    TT = 256
    full = lambda shape: pl.BlockSpec(shape, lambda b, t: (0,) * len(shape))
    return pl.pallas_call(
        functools.partial(_gdn_kernel, tt_rows=TT),
        grid=(B, T // TT),
        in_specs=[pl.BlockSpec((1, TT, 3 * B_WIDTH), lambda b, t: (b, t, 0)),
                  pl.BlockSpec((1, TT, B_WIDTH), lambda b, t: (b, t, 0)),
                  pl.BlockSpec((1, TT, 2 * B_HEADS), lambda b, t: (b, t, 0)),
                  pl.BlockSpec((1, 2 * B_HEADS, TT), lambda b, t: (b, 0, t)),
                  full((CONV_WIDTH, 3 * B_WIDTH)),
                  full((1, B_HEADS)), full((1, B_HEADS)), full((B_HEADS, 1)), full((B_HEADS, 1)),
                  full((1, B_HEAD_DIM))],
        out_specs=pl.BlockSpec((1, TT, B_WIDTH), lambda b, t: (b, t, 0)),
        out_shape=jax.ShapeDtypeStruct((B, T, B_WIDTH), BF16),
        scratch_shapes=[pltpu.VMEM((TT + 8, 3 * B_WIDTH), F32),
                        pltpu.VMEM((B_HEADS, B_HEAD_DIM, B_HEAD_DIM), F32)],
        compiler_params=_cparams(("parallel", "arbitrary")),
        name="gdn",
    )(xqkv, z, ab, abt, conv_w, a_log.reshape(1, -1), dt_bias.reshape(1, -1),
      a_log.reshape(-1, 1), dt_bias.reshape(-1, 1), gdn_norm.reshape(1, -1))


def _dsa_prep_kernel(ckv_ref, small_ref, kvn_ref, lng_ref, lnb_ref, cn_ref, cnt_ref, kin_ref):
    c = ckv_ref[0]
    ms = jnp.mean(c * c, axis=-1, keepdims=True)
    cn = c * lax.rsqrt(ms + RMS_EPS) * kvn_ref[...]
    cn_ref[0] = cn.astype(cn_ref.dtype)
    ones_row = jnp.where(lax.broadcasted_iota(jnp.int32, (DSA_ACC_PAD, DSA_KW), 0) == 0, 1.0, 0.0)
    for r in range(cn.shape[0] // DSA_KW):
        ct = jnp.concatenate([cn[r * DSA_KW:(r + 1) * DSA_KW].T, ones_row], axis=0)
        cnt_ref[0, r] = ct.astype(cnt_ref.dtype)
    ki = small_ref[0, :, SMALL_KI:SMALL_KI + IDX_DIM]
    kin_ref[0] = _layer_norm(ki, lng_ref[...], lnb_ref[...]).astype(kin_ref.dtype)


def _dsa_prep(ckv, small, kv_norm, ln_g, ln_b):
    B, T, _ = ckv.shape
    tm = 1024
    vec = lambda n: pl.BlockSpec((1, n), lambda b, i: (0, 0))
    return pl.pallas_call(
        _dsa_prep_kernel,
        grid=(B, T // tm),
        in_specs=[pl.BlockSpec((1, tm, KV_RANK), lambda b, i: (b, i, 0)),
                  pl.BlockSpec((1, tm, SMALL_W), lambda b, i: (b, i, 0)),
                  vec(KV_RANK), vec(IDX_DIM), vec(IDX_DIM)],
        out_specs=[pl.BlockSpec((1, tm, KV_RANK), lambda b, i: (b, i, 0)),
                   pl.BlockSpec((1, tm // DSA_KW, KV_RANK + DSA_ACC_PAD, DSA_KW), lambda b, i: (b, i, 0, 0)),
                   pl.BlockSpec((1, tm, IDX_DIM), lambda b, i: (b, i, 0))],
        out_shape=[jax.ShapeDtypeStruct((B, T, KV_RANK), BF16),
                   jax.ShapeDtypeStruct((B, T // DSA_KW, KV_RANK + DSA_ACC_PAD, DSA_KW), BF16),
                   jax.ShapeDtypeStruct((B, T, IDX_DIM), BF16)],
        compiler_params=_cparams(("parallel", "parallel")),
        name="dsa_prep",
    )(ckv, small, kv_norm.reshape(1, -1), ln_g.reshape(1, -1), ln_b.reshape(1, -1))


def _key_bits(k):
    return jnp.where(k >= 0, k, k ^ 0x7FFFFFFF)


def _key_to_f32(k):
    return lax.bitcast_convert_type(_key_bits(k), F32)


def _dsa_kernel(qc_ref, qi_ref, small_ref, cn_ref, cnt_ref, kin_ref, wuk_ref, wuvt_ref, bias_ref,
                o_ref, sc_s, scb_s, acc_s, qlt_s, m_s, sa_s, sb_s, ma_s, mb_s, *, topk):
    j = pl.program_id(1)
    QB, KW, H, R, dh, DI = DSA_QB, DSA_KW, C_HEADS, KV_RANK, C_HEAD_DIM, IDX_DIM
    nck = (j * QB + QB - 1) // KW + 1

    qct = qc_ref[0].astype(F32).T.astype(BF16)
    for h in range(H):
        qlt_s[:, h * QB:(h + 1) * QB] = _dot(wuk_ref[h], qct[h * dh:(h + 1) * dh]).astype(BF16)

    qit = qi_ref[0].astype(F32).T.astype(BF16)
    qi_all = jnp.concatenate([qit[h * DI:(h + 1) * DI] for h in range(IDX_HEADS)], axis=1)
    wt = small_ref[0].T
    w_all = jnp.concatenate([wt[SMALL_WI + h:SMALL_WI + h + 1] for h in range(IDX_HEADS)],
                            axis=1) * IDX_WEIGHT_SCALE
    tk0 = lax.broadcasted_iota(jnp.int32, (KW, QB), 0)
    tq = j * QB + lax.broadcasted_iota(jnp.int32, (KW, QB), 1)

    def idx_raw(c):
        rows = pl.ds(pl.multiple_of(c * KW, KW), KW)
        return _dot(kin_ref[0, rows, :], qi_all)

    def idx_store(c, raw):
        s8 = jnp.maximum(raw, 0.0) * w_all
        sc = s8[:, 0:QB]
        for h in range(1, IDX_HEADS):
            sc = sc + s8[:, h * QB:(h + 1) * QB]
        sc = jnp.where(tk0 + c * KW <= tq, sc, -jnp.inf)
        sc_s[c] = sc
        top = lax.bitcast_convert_type(sc, jnp.int32) & TOP16
        scb_s[c] = lax.bitcast_convert_type(top, F32).astype(BF16)

    def idx_body(p, _):
        raw0, raw1 = idx_raw(2 * p), idx_raw(2 * p + 1)
        idx_store(2 * p, raw0)
        idx_store(2 * p + 1, raw1)
        return 0

    npair = (nck + 1) // 2
    lax.fori_loop(0, npair, idx_body, 0)

    ngrp = (nck + DSA_CNT_UNROLL - 1) // DSA_CNT_UNROLL

    def fill_body(c, _):
        sc_s[c] = jnp.full((KW, QB), -jnp.inf, F32)
        scb_s[c] = jnp.full((KW, QB), -jnp.inf, BF16)
        return 0

    lax.fori_loop(2 * npair, ngrp * DSA_CNT_UNROLL, fill_body, 0)

    tcol = j * QB + lax.broadcasted_iota(jnp.int32, (1, QB), 1)
    kk = jnp.minimum(tcol + 1, topk).astype(F32)

    def midpoint(lo, hi):
        return (lo >> 1) + (hi >> 1) + (lo & hi & 1) + ((lo ^ hi) & 1)

    def step(carry, mid, cnt):
        lo, hi, cge, cgt = carry
        ok = cnt >= kk
        return (jnp.where(ok, mid, lo), jnp.where(ok, hi, mid - 1),
                jnp.where(ok, cnt, cge), jnp.where(ok, cgt, cnt))

    one_b, zero_b = jnp.ones((), BF16), jnp.zeros((), BF16)

    def coarse_body(_, carry):
        mid = midpoint(carry[0], carry[1])
        midb = lax.bitcast_convert_type(_key_bits(mid << 16) & TOP16, F32).astype(BF16)

        def cnt_body(gi, acc):
            for u in range(DSA_CNT_UNROLL):
                hit = jnp.where(scb_s[gi * DSA_CNT_UNROLL + u] >= midb, one_b, zero_b)
                parts = [hit[r * 16:(r + 1) * 16] for r in range(KW // 16)]
                while len(parts) > 1:
                    parts = [parts[i] + parts[i + 1] for i in range(0, len(parts), 2)]
                acc = acc + parts[0].astype(F32)
            return acc

        acc = lax.fori_loop(0, ngrp, cnt_body, jnp.zeros((16, QB), F32))
        return step(carry, mid, jnp.sum(acc, axis=0, keepdims=True))

    def fine_body(_, carry):
        mid = midpoint(carry[0], carry[1])
        midf = _key_to_f32(mid)

        def cnt_body(gi, acc):
            for u in range(DSA_CNT_UNROLL):
                hit = jnp.where(sc_s[gi * DSA_CNT_UNROLL + u] >= midf, 1.0, 0.0)
                acc = acc + jnp.sum(hit.reshape(KW // 8, 8, QB), axis=0)
            return acc

        acc = lax.fori_loop(0, ngrp, cnt_body, jnp.zeros((8, QB), F32))
        return step(carry, mid, jnp.sum(acc, axis=0, keepdims=True))

    init = (jnp.full((1, QB), KEY_LO >> 16, jnp.int32), jnp.full((1, QB), KEY_HI >> 16, jnp.int32),
            (tcol + 1).astype(F32), jnp.zeros((1, QB), F32))
    lo16, _, cge, cgt = lax.fori_loop(0, 16, coarse_body, init)
    state = (lo16 << 16, (lo16 << 16) | 0xFFFF, cge, cgt)

    def all_exact(c):
        return (jnp.min(jnp.where(c[2] == kk, 1.0, 0.0)) > 0.5).astype(jnp.int32)

    def tail_cond(st):
        return jnp.logical_and(st[0] < 16, st[1] == 0)

    def tail_body(st):
        c = st[2:]
        for _ in range(DSA_BIS_GROUP):
            c = fine_body(0, c)
        return (st[0] + DSA_BIS_GROUP, all_exact(c)) + tuple(c)

    st = lax.while_loop(tail_cond, tail_body, (jnp.int32(0), all_exact(state)) + tuple(state))
    lo, cge, cgt = st[2], st[4], st[5]
    tau = _key_to_f32(lo)

    @pl.when(jnp.max(cge - kk) > 0)
    def _():
        need = kk - cgt
        lr = lax.broadcasted_iota(jnp.int32, (KW, KW), 0)
        lc = lax.broadcasted_iota(jnp.int32, (KW, KW), 1)
        lower = jnp.where(lc <= lr, 1.0, 0.0).astype(BF16)

        def tie_body(c, seen):
            s = sc_s[c]
            eq = s == tau
            pc = _dot(lower, jnp.where(eq, 1.0, 0.0).astype(BF16)) + seen
            sc_s[c] = jnp.where(jnp.logical_and(eq, pc > need), -jnp.inf, s)
            return pc[KW - 1:KW, :]

        lax.fori_loop(0, nck, tie_body, jnp.zeros((1, QB), F32))

    m_s[...] = jnp.full(m_s.shape, M_INIT, F32)
    acc_s[...] = jnp.zeros(acc_s.shape, F32)
    last = sc_s.shape[0] - 1

    def produce(c, s_ref, mx_ref):
        cc = jnp.minimum(c, last)
        rows = pl.ds(pl.multiple_of(cc * KW, KW), KW)
        madd = jnp.where(sc_s[cc] >= tau, 0.0, NEG)
        tile = jnp.clip((j * QB - cc * KW) // QB, 0, bias_ref.shape[0] - 1)
        s = _dot(cn_ref[0, rows, :], qlt_s[...]) + (bias_ref[tile] + jnp.concatenate([madd] * H, axis=1))
        s_ref[...] = s
        mx_ref[...] = jnp.max(s, axis=0, keepdims=True)

    def consume(c, s_ref, mx_ref):
        m_old = m_s[...]
        m_new = jnp.maximum(m_old, mx_ref[...])
        pr = jnp.exp2(s_ref[...] - m_new)
        m_s[...] = m_new
        acc_s[...] = (jnp.exp2(m_old - m_new) * acc_s[...]
                      + _dot(cnt_ref[0, jnp.minimum(c, last)], pr.astype(BF16)))

    produce(0, sa_s, ma_s)

    def pair_body(i, _):
        produce(2 * i + 1, sb_s, mb_s)
        consume(2 * i, sa_s, ma_s)
        produce(2 * i + 2, sa_s, ma_s)
        consume(2 * i + 1, sb_s, mb_s)
        return 0

    tail = (nck + 1) // 2 - 1
    lax.fori_loop(0, tail, pair_body, 0)
    produce(2 * tail + 1, sb_s, mb_s)
    consume(2 * tail, sa_s, ma_s)
    consume(2 * tail + 1, sb_s, mb_s)

    inv = 1.0 / acc_s[R:R + 1, :]
    octs = []
    for h in range(H):
        hs = slice(h * QB, (h + 1) * QB)
        octs.append(_dot(wuvt_ref[h], (acc_s[0:R, hs] * inv[:, hs]).astype(BF16)))
    o_ref[0] = jnp.concatenate(octs, axis=0).T.astype(o_ref.dtype)


def _dsa(q_c, qi, small, cn, cnt, kin, wuk, wuvt, bias):
    B, T, _ = q_c.shape
    QB, KW = DSA_QB, DSA_KW
    W = C_HEADS * QB
    topk = min(DSA_TOPK_MAX, T // 4)
    assert (T // KW) % DSA_CNT_UNROLL == 0
    row = lambda w: pl.BlockSpec((1, QB, w), lambda b, j: (b, j, 0))
    return pl.pallas_call(
        functools.partial(_dsa_kernel, topk=topk),
        grid=(B, T // QB),
        in_specs=[row(C_WIDTH), row(IDX_HEADS * IDX_DIM), row(SMALL_W),
                  pl.BlockSpec((1, T, KV_RANK), lambda b, j: (b, 0, 0)),
                  pl.BlockSpec((1,) + cnt.shape[1:], lambda b, j: (b, 0, 0, 0)),
                  pl.BlockSpec((1, T, IDX_DIM), lambda b, j: (b, 0, 0)),
                  pl.BlockSpec(wuk.shape, lambda b, j: (0, 0, 0)),
                  pl.BlockSpec(wuvt.shape, lambda b, j: (0, 0, 0)),
                  pl.BlockSpec(bias.shape, lambda b, j: (0, 0, 0))],
        out_specs=row(C_WIDTH),
        out_shape=jax.ShapeDtypeStruct((B, T, C_WIDTH), BF16),
        scratch_shapes=[pltpu.VMEM((T // KW, KW, QB), F32),
                        pltpu.VMEM((T // KW, KW, QB), BF16),
                        pltpu.VMEM((KV_RANK + DSA_ACC_PAD, W), F32),
                        pltpu.VMEM((KV_RANK, W), BF16),
                        pltpu.VMEM((1, W), F32),
                        pltpu.VMEM((KW, W), F32),
                        pltpu.VMEM((KW, W), F32),
                        pltpu.VMEM((1, W), F32),
                        pltpu.VMEM((1, W), F32)],
        compiler_params=_cparams(("parallel", "arbitrary")),
        name="dsa",
    )(q_c, qi, small, cn, cnt, kin, wuk, wuvt, bias)


def _merge_kernel(x_ref, oa_ref, ob_ref, oc_ref, g_ref, wa_ref, wb_ref, wc_ref, wo_ref, lg_ref, lb_ref,
                  out_ref):
    D = D_MODEL
    y = _sigmoid(g_ref[:, 0:D].astype(F32)) * _dot(oa_ref[...], wa_ref[...])
    y = y + _sigmoid(g_ref[:, D:2 * D].astype(F32)) * _dot(ob_ref[...], wb_ref[...])
    y = y + _sigmoid(g_ref[:, 2 * D:3 * D].astype(F32)) * _dot(oc_ref[...], wc_ref[...])
    mix = _dot(y.astype(BF16), wo_ref[...])
    out_ref[...] = _layer_norm(DEEPNORM_ALPHA * x_ref[...] + mix, lg_ref[...], lb_ref[...])


def _merge(x2, oa, ob, oc, gates, wa, wb, wc, wo, ln_g, ln_b):
    n = x2.shape[0]
    tm = 512
    row = lambda w: pl.BlockSpec((tm, w), lambda i: (i, 0))
    full = lambda a: pl.BlockSpec(a.shape, lambda i: (0, 0))
    lg, lb = ln_g.reshape(1, -1), ln_b.reshape(1, -1)
    return pl.pallas_call(
        _merge_kernel,
        grid=(n // tm,),
        in_specs=[row(D_MODEL), row(A_WIDTH), row(B_WIDTH), row(C_WIDTH), row(3 * D_MODEL),
                  full(wa), full(wb), full(wc), full(wo), full(lg), full(lb)],
        out_specs=row(D_MODEL),
        out_shape=jax.ShapeDtypeStruct((n, D_MODEL), F32),
        compiler_params=_cparams(("parallel",)),
        name="merge",
    )(x2, oa, ob, oc, gates, wa, wb, wc, wo, lg, lb)


def _mlp_kernel(x_ref, wu_ref, wd_ref, lg_ref, lb_ref, out_ref):
    x = x_ref[...]
    xb = x.astype(BF16)
    acc = jnp.zeros(x.shape, F32)
    fc = 1024
    for c in range(D_FF // fc):
        h = jnp.maximum(_dot(xb, wu_ref[:, c * fc:(c + 1) * fc]), 0.0)
        acc = acc + _dot((h * h).astype(BF16), wd_ref[c * fc:(c + 1) * fc, :])
    out_ref[...] = _layer_norm(DEEPNORM_ALPHA * x + acc, lg_ref[...], lb_ref[...])


def _mlp(x2, wu, wd, ln_g, ln_b):
    n = x2.shape[0]
    tm = 512
    lg, lb = ln_g.reshape(1, -1), ln_b.reshape(1, -1)
    return pl.pallas_call(
        _mlp_kernel,
        grid=(n // tm,),
        in_specs=[pl.BlockSpec((tm, D_MODEL), lambda i: (i, 0)),
                  pl.BlockSpec(wu.shape, lambda i: (0, 0), pipeline_mode=pl.Buffered(1)),
                  pl.BlockSpec(wd.shape, lambda i: (0, 0), pipeline_mode=pl.Buffered(1)),
                  pl.BlockSpec(lg.shape, lambda i: (0, 0)),
                  pl.BlockSpec(lb.shape, lambda i: (0, 0))],
        out_specs=pl.BlockSpec((tm, D_MODEL), lambda i: (i, 0)),
        out_shape=jax.ShapeDtypeStruct((n, D_MODEL), F32),
        compiler_params=_cparams(("parallel",)),
        name="mlp",
    )(x2, wu, wd, lg, lb)


def _prep_w_uk(w_uk):
    return (w_uk * (w_uk.shape[-1] ** -0.5 * LOG2E)).transpose(1, 0, 2).astype(BF16)


def _prep_w_uv(w_uv):
    return w_uv.transpose(1, 2, 0).astype(BF16)


def _mixer_branches(x2, B, T, w_pad, conv_w, a_log, dt_bias, gdn_norm, kv_norm, ln_g, ln_b,
                    wuk, wuvt, tab, bias_a, bias_c):
    qkv_a, qkv_b, z_b, q_c, ckv, qi, small, gates = _in_proj(x2, w_pad)

    o_a = _moba(qkv_a.reshape(B, T, -1), tab, bias_a).reshape(B * T, A_WIDTH)

    ab = small[:, SMALL_A:SMALL_A + 2 * B_HEADS].reshape(B, T, 2 * B_HEADS)
    o_b = _gdn(qkv_b.reshape(B, T, -1), z_b.reshape(B, T, -1), ab, ab.transpose(0, 2, 1),
               conv_w, a_log, dt_bias, gdn_norm).reshape(B * T, B_WIDTH)

    small3 = small.reshape(B, T, SMALL_W)
    cn, cnt, kin = _dsa_prep(ckv.reshape(B, T, KV_RANK), small3, kv_norm, ln_g, ln_b)
    o_c = _dsa(q_c.reshape(B, T, -1), qi.reshape(B, T, -1), small3, cn, cnt, kin,
               wuk, wuvt, bias_c).reshape(B * T, C_WIDTH)
    return o_a, o_b, o_c, gates


def kernel(x, rel_bias, w_in, conv_w, a_log, dt_bias, gdn_norm, kv_norm, idx_k_ln_g, idx_k_ln_b, w_uk, w_uv, w_branch_a, w_branch_b, w_branch_c, w_out, ln1_g, ln1_b, w_up, w_down, ln2_g, ln2_b):
    B, T, D = x.shape
    bias_a, bias_c = _bias_tiles(rel_bias)
    tab = rel_bias.T
    x2 = x.reshape(B * T, D)
    for i in range(w_in.shape[0]):
        o_a, o_b, o_c, gates = _mixer_branches(
            x2, B, T, _prep_w_in(w_in[i]), conv_w[i], a_log[i], dt_bias[i], gdn_norm[i], kv_norm[i],
            idx_k_ln_g[i], idx_k_ln_b[i], _prep_w_uk(w_uk[i]), _prep_w_uv(w_uv[i]),
            tab, bias_a, bias_c)
        x2 = _merge(x2, o_a, o_b, o_c, gates, w_branch_a[i].astype(BF16), w_branch_b[i].astype(BF16),
                    w_branch_c[i].astype(BF16), w_out[i].astype(BF16), ln1_g[i], ln1_b[i])
        x2 = _mlp(x2, w_up[i].astype(BF16), w_down[i].astype(BF16), ln2_g[i], ln2_b[i])
    return x2.reshape(B, T, D)
```
